```python
import jax, jax.numpy as jnp
from jax import lax
import numpy as np

D_MODEL = 1024
BATCH = 2
SEQ = 8192
DEPTH = 4
DEC_BATCH = 16
DEC_SEQ = 16
PAST_LEN = 2048

CHUNK = 64
W_A = 512
CONV_A = 31
W_B = 512
CONV_B = 3
W_C = 512
H_C = 4
C_HEAD = W_C // H_C
MLP_CHUNK = 128
N_BRANCH = 3
D_FF = 2816
D_PLE = 256
EPS = 1e-6
SPLIT_SIZES = (W_A, W_A, W_B, W_B, W_B, W_C, W_C, N_BRANCH * D_MODEL)
N_IN = sum(SPLIT_SIZES)

kernel_name = "hybrid_conv_gmlp_streaming_encoder_step"


def rmsnorm(x, g):
    xf = x.astype(jnp.float32)
    y = xf * lax.rsqrt(jnp.mean(xf * xf, axis=-1, keepdims=True) + EPS)
    return (y * g.astype(jnp.float32)).astype(x.dtype)


def layernorm(x, g, b):
    xf = x.astype(jnp.float32)
    mu = jnp.mean(xf, axis=-1, keepdims=True)
    xc = xf - mu
    y = xc * lax.rsqrt(jnp.mean(xc * xc, axis=-1, keepdims=True) + EPS)
    return (y * g.astype(jnp.float32) + b.astype(jnp.float32)).astype(x.dtype)


def swiglu(x, wg, wu, wd):
    return (jax.nn.silu(x @ wg) * (x @ wu)) @ wd


def causal_dwconv(hist, z, w):
    k = w.shape[0]
    zp = jnp.concatenate([hist, z], axis=1)
    out = lax.conv_general_dilated(zp, w[:, None, :], window_strides=(1,), padding='VALID',
                                   dimension_numbers=('NWC', 'WIO', 'NWC'),
                                   feature_group_count=z.shape[-1])
    return out, zp[:, zp.shape[1] - (k - 1):]


def chunk_spatial_gate(u, v, w_s, b_s):
    n, l, _ = v.shape
    t = min(l, MLP_CHUNK)
    mask = jnp.tril(jnp.ones((t, t), dtype=bool))
    wm = jnp.where(mask, w_s[:, :t, :t], jnp.zeros((), w_s.dtype))
    vc = v.reshape(n, l // t, t, H_C, C_HEAD)
    mixed = jnp.einsum('hts,bnshc->bnthc', wm, vc) + b_s[:, :t].T[None, None, :, :, None]
    return u * mixed.reshape(n, l, W_C)


def layer(x, p, hist_a, hist_b, w, i):
    x = x + 0.5 * swiglu(rmsnorm(x, w['ffn1_norm'][i]), w['ffn1_w_gate'][i], w['ffn1_w_up'][i], w['ffn1_w_down'][i])
    h = rmsnorm(x, w['mix_norm'][i])
    proj = h @ w['w_in'][i]
    a_val, a_gate, b_bg, b_cg, b_val, c_u, c_v, gates = jnp.split(
        proj, [int(s) for s in np.cumsum(SPLIT_SIZES)[:-1]], axis=-1)
    za = a_val * jax.nn.sigmoid(a_gate)
    ca, new_a = causal_dwconv(hist_a, za, w['conv_a_w'][i])
    ca = ca + w['conv_a_b'][i]
    ya = jax.nn.silu(layernorm(ca, w['ln_a_g'][i], w['ln_a_b'][i])) @ w['w_a_out'][i]
    cb, new_b = causal_dwconv(hist_b, b_cg * b_val, w['conv_b_w'][i])
    yb = (b_bg * cb) @ w['w_b_out'][i]
    vn = layernorm(jax.nn.gelu(c_v), w['ln_c_g'][i], w['ln_c_b'][i])
    yc = chunk_spatial_gate(jax.nn.gelu(c_u), vn, w['w_spatial'][i], w['b_spatial'][i]) @ w['w_c_out'][i]
    g = jax.nn.sigmoid(gates + w['gate_bias'][i]).reshape(gates.shape[:-1] + (N_BRANCH, D_MODEL))
    merged = g[..., 0, :] * ya + g[..., 1, :] * yb + g[..., 2, :] * yc
    x = x + merged @ w['w_o'][i]
    x = x + 0.5 * swiglu(rmsnorm(x, w['ffn2_norm'][i]), w['ffn2_w_gate'][i], w['ffn2_w_up'][i], w['ffn2_w_down'][i])
    x = x + jax.nn.sigmoid(rmsnorm(x, w['ple_norm'][i]) @ w['w_ple_gate'][i]) * (p @ w['w_ple_proj'][i])
    return x, new_a, new_b, vn


def trunk(x, p, hist_a, hist_b, w):
    sa, sb, sv = [], [], []
    for i in range(DEPTH):
        x, na, nb, vn = layer(x, p[i], hist_a[i], hist_b[i], w, i)
        sa.append(na)
        sb.append(nb)
        sv.append(vn)
    return rmsnorm(x, w['final_norm']), jnp.stack(sa), jnp.stack(sb), jnp.stack(sv)


def setup_inputs(seed: int = 0) -> dict:
    key = jax.random.key(seed)
    ks = iter(jax.random.split(key, 64))
    nrm = lambda shape, s: jax.random.normal(next(ks), shape, jnp.float32) * s
    gain = lambda shape: 1.0 + nrm(shape, 0.05)
    d = {}
    d['x_prompt'] = nrm((BATCH, SEQ, D_MODEL), 1.0)
    d['x_sample'] = nrm((DEC_BATCH, DEC_SEQ, D_MODEL), 1.0)
    d['p_prompt'] = nrm((DEPTH, BATCH, SEQ, D_PLE), 1.0)
    d['p_sample'] = nrm((DEPTH, DEC_BATCH, DEC_SEQ, D_PLE), 1.0)
    d['cache_conv_a'] = nrm((DEPTH, DEC_BATCH, CONV_A - 1, W_A), 0.5)
    d['cache_conv_b'] = nrm((DEPTH, DEC_BATCH, CONV_B - 1, W_B), 0.5)
    d['ffn1_norm'] = gain((DEPTH, D_MODEL))
    d['ffn1_w_gate'] = nrm((DEPTH, D_MODEL, D_FF), D_MODEL ** -0.5)
    d['ffn1_w_up'] = nrm((DEPTH, D_MODEL, D_FF), D_MODEL ** -0.5)
    d['ffn1_w_down'] = nrm((DEPTH, D_FF, D_MODEL), D_FF ** -0.5)
    d['mix_norm'] = gain((DEPTH, D_MODEL))
    d['w_in'] = nrm((DEPTH, D_MODEL, N_IN), D_MODEL ** -0.5)
    d['gate_bias'] = nrm((DEPTH, N_BRANCH * D_MODEL), 0.02)
    d['conv_a_w'] = nrm((DEPTH, CONV_A, W_A), CONV_A ** -0.5)
    d['conv_a_b'] = nrm((DEPTH, W_A), 0.02)
    d['ln_a_g'] = gain((DEPTH, W_A))
    d['ln_a_b'] = nrm((DEPTH, W_A), 0.02)
    d['w_a_out'] = nrm((DEPTH, W_A, D_MODEL), W_A ** -0.5)
    d['conv_b_w'] = nrm((DEPTH, CONV_B, W_B), CONV_B ** -0.5)
    d['w_b_out'] = nrm((DEPTH, W_B, D_MODEL), W_B ** -0.5)
    d['ln_c_g'] = gain((DEPTH, W_C))
    d['ln_c_b'] = nrm((DEPTH, W_C), 0.02)
    d['w_spatial'] = nrm((DEPTH, H_C, MLP_CHUNK, MLP_CHUNK), MLP_CHUNK ** -0.5)
    d['b_spatial'] = 1.0 + nrm((DEPTH, H_C, MLP_CHUNK), 0.02)
    d['w_c_out'] = nrm((DEPTH, W_C, D_MODEL), W_C ** -0.5)
    d['w_o'] = nrm((DEPTH, D_MODEL, D_MODEL), D_MODEL ** -0.5)
    d['ffn2_norm'] = gain((DEPTH, D_MODEL))
    d['ffn2_w_gate'] = nrm((DEPTH, D_MODEL, D_FF), D_MODEL ** -0.5)
    d['ffn2_w_up'] = nrm((DEPTH, D_MODEL, D_FF), D_MODEL ** -0.5)
    d['ffn2_w_down'] = nrm((DEPTH, D_FF, D_MODEL), D_FF ** -0.5)
    d['ple_norm'] = gain((DEPTH, D_MODEL))
    d['w_ple_gate'] = nrm((DEPTH, D_MODEL, D_MODEL), D_MODEL ** -0.5)
    d['w_ple_proj'] = nrm((DEPTH, D_PLE, D_MODEL), D_PLE ** -0.5)
    d['final_norm'] = gain((D_MODEL,))
    return d


def reference(x_prompt, x_sample, p_prompt, p_sample, cache_conv_a, cache_conv_b,
              ffn1_norm, ffn1_w_gate, ffn1_w_up, ffn1_w_down, mix_norm, w_in, gate_bias,
              conv_a_w, conv_a_b, ln_a_g, ln_a_b, w_a_out, conv_b_w, w_b_out,
              ln_c_g, ln_c_b, w_spatial, b_spatial, w_c_out, w_o,
              ffn2_norm, ffn2_w_gate, ffn2_w_up, ffn2_w_down,
              ple_norm, w_ple_gate, w_ple_proj, final_norm):
    w = dict(ffn1_norm=ffn1_norm, ffn1_w_gate=ffn1_w_gate, ffn1_w_up=ffn1_w_up, ffn1_w_down=ffn1_w_down,
             mix_norm=mix_norm, w_in=w_in, gate_bias=gate_bias,
             conv_a_w=conv_a_w, conv_a_b=conv_a_b, ln_a_g=ln_a_g, ln_a_b=ln_a_b, w_a_out=w_a_out,
             conv_b_w=conv_b_w, w_b_out=w_b_out,
             ln_c_g=ln_c_g, ln_c_b=ln_c_b, w_spatial=w_spatial, b_spatial=b_spatial, w_c_out=w_c_out,
             w_o=w_o, ffn2_norm=ffn2_norm, ffn2_w_gate=ffn2_w_gate, ffn2_w_up=ffn2_w_up,
             ffn2_w_down=ffn2_w_down, ple_norm=ple_norm, w_ple_gate=w_ple_gate, w_ple_proj=w_ple_proj,
             final_norm=final_norm)
    b = x_prompt.shape[0]
    zero_a = jnp.zeros((DEPTH, b, CONV_A - 1, W_A), x_prompt.dtype)
    zero_b = jnp.zeros((DEPTH, b, CONV_B - 1, W_B), x_prompt.dtype)
    y_prompt, conv_a_prompt, conv_b_prompt, _ = trunk(x_prompt, p_prompt, zero_a, zero_b, w)
    y_sample, conv_a_sample, conv_b_sample, chunk_v_sample = trunk(x_sample, p_sample, cache_conv_a, cache_conv_b, w)
    return (y_prompt, y_sample, conv_a_prompt, conv_a_sample, conv_b_prompt, conv_b_sample, chunk_v_sample)
```

```python
import functools

import jax
import jax.numpy as jnp
from jax import lax
from jax.experimental import pallas as pl
from jax.experimental.pallas import tpu as pltpu

D_MODEL = 1024
D_FF = 2816
D_PLE = 256
W_BR = 512
CONV_A = 31
CONV_B = 3
H_C = 4
C_HEAD = W_BR // H_C
MLP_CHUNK = 128
EPS = 1e-6
COL_A, COL_B, COL_C, COL_G, COL_END = 0, 1024, 2560, 3584, 6656

HIST_A_PAD = 32
HIST_B_PAD = 8
CONV_STRIP_VREGS = 16
VMEM_LIMIT_BYTES = 56 * 1024 * 1024

BF16 = jnp.bfloat16
F32 = jnp.float32


def _dot(a, b):
    return jnp.dot(a, b, preferred_element_type=F32)


def _rmsnorm(x, g):
    return x * lax.rsqrt(jnp.mean(x * x, axis=-1, keepdims=True) + EPS) * g


def _layernorm(x, g, b):
    mu = jnp.mean(x, axis=-1, keepdims=True)
    xc = x - mu
    return xc * lax.rsqrt(jnp.mean(xc * xc, axis=-1, keepdims=True) + EPS) * g + b


def _swiglu_half_step(x, g, wg_ref, wu_ref, wd_ref):
    h = _rmsnorm(x, g).astype(BF16)
    act = jax.nn.silu(_dot(h, wg_ref[...])) * _dot(h, wu_ref[...])
    return x + 0.5 * _dot(act.astype(BF16), wd_ref[...])


def _ffn_kernel(x_ref, g_ref, wg_ref, wu_ref, wd_ref, o_ref):
    o_ref[...] = _swiglu_half_step(x_ref[...], g_ref[...], wg_ref, wu_ref, wd_ref)


def _ffn_ple_kernel(final, x_ref, p_ref, g_ref, wg_ref, wu_ref, wd_ref,
                    pg_ref, wpg_ref, wpp_ref, fg_ref, o_ref):
    x = _swiglu_half_step(x_ref[...], g_ref[...], wg_ref, wu_ref, wd_ref)
    gate = jax.nn.sigmoid(_dot(_rmsnorm(x, pg_ref[...]).astype(BF16), wpg_ref[...]))
    x = x + gate * _dot(p_ref[...].astype(BF16), wpp_ref[...])
    if final:
        x = _rmsnorm(x, fg_ref[...])
    o_ref[...] = x


def _causal_dwconv(zp_ref, w_ref, n_taps, first_row, n_seq, n_rows):
    rows = min(n_rows, 32)
    seqs = max(1, min(n_seq, CONV_STRIP_VREGS * 8 * 128 // (rows * W_BR)))
    w = w_ref[...]
    seq_parts = []
    for s0 in range(0, n_seq, seqs):
        row_parts = []
        for r0 in range(0, n_rows, rows):
            acc = jnp.zeros((seqs, rows, W_BR), F32)
            for k in range(n_taps):
                acc = acc + w[k] * zp_ref[pl.ds(s0, seqs), pl.ds(first_row + r0 + k, rows), :]
            row_parts.append(acc)
        seq_parts.append(row_parts[0] if len(row_parts) == 1 else jnp.concatenate(row_parts, axis=1))
    return seq_parts[0] if len(seq_parts) == 1 else jnp.concatenate(seq_parts, axis=0)


def _spatial_mix(vn, wsp_ref, bsp_ref, n_seq, n_rows):
    t = min(n_rows, MLP_CHUNK)
    r_total = n_seq * n_rows
    tril = lax.broadcasted_iota(jnp.int32, (t, t), 0) >= lax.broadcasted_iota(jnp.int32, (t, t), 1)
    heads = []
    if t == MLP_CHUNK:
        n_chunks = r_total // t
        bias = bsp_ref[...]
        for h in range(H_C):
            cols = slice(h * C_HEAD, (h + 1) * C_HEAD)
            wm = jnp.where(tril, wsp_ref[h], 0.0).astype(BF16)
            rhs = jnp.concatenate([vn[j * t:(j + 1) * t, cols] for j in range(n_chunks)], axis=1)
            mh = _dot(wm, rhs) + bias[:, h:h + 1]
            heads.append(jnp.concatenate([mh[:, j * C_HEAD:(j + 1) * C_HEAD] for j in range(n_chunks)], axis=0))
    else:
        sel = (lax.broadcasted_iota(jnp.int32, (r_total, t), 0) % t
               == lax.broadcasted_iota(jnp.int32, (r_total, t), 1)).astype(F32)
        sel_t = (lax.broadcasted_iota(jnp.int32, (t, r_total), 1) % t
                 == lax.broadcasted_iota(jnp.int32, (t, r_total), 0)).astype(F32)
        same_seq = (lax.broadcasted_iota(jnp.int32, (r_total, r_total), 0) // t
                    == lax.broadcasted_iota(jnp.int32, (r_total, r_total), 1) // t)
        bias = jnp.concatenate([bsp_ref[0:t, :]] * n_seq, axis=0)
        for h in range(H_C):
            cols = slice(h * C_HEAD, (h + 1) * C_HEAD)
            wm = jnp.where(tril, wsp_ref[h, 0:t, 0:t], 0.0)
            big = jnp.where(same_seq, _dot(_dot(sel, wm), sel_t), 0.0).astype(BF16)
            heads.append(_dot(big, vn[:, cols]) + bias[:, h:h + 1])
    return jnp.concatenate(heads, axis=1)


def _mixer_kernel(n_seq, n_rows, x_ref, hista_ref, histb_ref, g_ref, win_ref, gbias_ref,
                  caw_ref, cab_ref, lnag_ref, lnab_ref, waout_ref, cbw_ref, wbout_ref,
                  lncg_ref, lncb_ref, wsp_ref, bsp_ref, wcout_ref, wo_ref,
                  o_ref, newa_ref, newb_ref, vn_ref, zpa_ref, zpb_ref):
    r_total = n_seq * n_rows

    @pl.when(pl.program_id(1) == 0)
    def _():
        zpa_ref[:, 0:HIST_A_PAD, :] = hista_ref[...]
        zpb_ref[:, 0:HIST_B_PAD, :] = histb_ref[...]

    x = x_ref[...]
    h = _rmsnorm(x, g_ref[...]).astype(BF16)

    pa = _dot(h, win_ref[:, COL_A:COL_B])
    za = pa[:, :W_BR] * jax.nn.sigmoid(pa[:, W_BR:])
    zpa_ref[:, HIST_A_PAD:HIST_A_PAD + n_rows, :] = za.reshape(n_seq, n_rows, W_BR)
    ca = _causal_dwconv(zpa_ref, caw_ref, CONV_A, HIST_A_PAD - (CONV_A - 1), n_seq, n_rows)
    ca = ca.reshape(r_total, W_BR) + cab_ref[...]
    ya = _dot(jax.nn.silu(_layernorm(ca, lnag_ref[...], lnab_ref[...])).astype(BF16), waout_ref[...])
    tail_a = zpa_ref[:, n_rows:n_rows + HIST_A_PAD, :]
    zpa_ref[:, 0:HIST_A_PAD, :] = tail_a
    newa_ref[...] = tail_a

    pb = _dot(h, win_ref[:, COL_B:COL_C])
    zb = pb[:, W_BR:2 * W_BR] * pb[:, 2 * W_BR:]
    zpb_ref[:, HIST_B_PAD:HIST_B_PAD + n_rows, :] = zb.reshape(n_seq, n_rows, W_BR)
    cb = _causal_dwconv(zpb_ref, cbw_ref, CONV_B, HIST_B_PAD - (CONV_B - 1), n_seq, n_rows)
    yb = _dot((pb[:, :W_BR] * cb.reshape(r_total, W_BR)).astype(BF16), wbout_ref[...])
    tail_b = zpb_ref[:, n_rows:n_rows + HIST_B_PAD, :]
    zpb_ref[:, 0:HIST_B_PAD, :] = tail_b
    newb_ref[...] = tail_b

    pc = _dot(h, win_ref[:, COL_C:COL_G])
    vn = _layernorm(jax.nn.gelu(pc[:, W_BR:]), lncg_ref[...], lncb_ref[...])
    vn_ref[...] = vn
    mixed = _spatial_mix(vn.astype(BF16), wsp_ref, bsp_ref, n_seq, n_rows)
    yc = _dot((jax.nn.gelu(pc[:, :W_BR]) * mixed).astype(BF16), wcout_ref[...])

    gates = jax.nn.sigmoid(_dot(h, win_ref[:, COL_G:COL_END]) + gbias_ref[...])
    merged = (gates[:, :D_MODEL] * ya + gates[:, D_MODEL:2 * D_MODEL] * yb
              + gates[:, 2 * D_MODEL:] * yc)
    o_ref[...] = x + _dot(merged.astype(BF16), wo_ref[...])


def _param_spec(layer, tail_shape, grid_rank):
    zeros = (0,) * len(tail_shape)
    if grid_rank == 1:
        index_map = lambda t: (layer,) + zeros
    else:
        index_map = lambda b, t: (layer,) + zeros
    return pl.BlockSpec((None,) + tuple(tail_shape), index_map, pipeline_mode=pl.Buffered(1))


def _compiler_params(grid_rank):
    return pltpu.CompilerParams(dimension_semantics=("arbitrary",) * grid_rank,
                                vmem_limit_bytes=VMEM_LIMIT_BYTES)


def _ffn_call(x, layer, w, tm):
    n = x.shape[0]
    row_spec = pl.BlockSpec((tm, D_MODEL), lambda t: (t, 0))
    return pl.pallas_call(
        _ffn_kernel,
        out_shape=jax.ShapeDtypeStruct(x.shape, F32),
        grid=(n // tm,),
        in_specs=[row_spec,
                  _param_spec(layer, (1, D_MODEL), 1),
                  _param_spec(layer, (D_MODEL, D_FF), 1),
                  _param_spec(layer, (D_MODEL, D_FF), 1),
                  _param_spec(layer, (D_FF, D_MODEL), 1)],
        out_specs=row_spec,
        compiler_params=_compiler_params(1),
        name="ffn1",
    )(x, w['ffn1_norm'], w['ffn1_w_gate'], w['ffn1_w_up'], w['ffn1_w_down'])


def _ffn_ple_call(x, p, layer, w, tm, final):
    n = x.shape[0]
    row_spec = pl.BlockSpec((tm, D_MODEL), lambda t: (t, 0))
    return pl.pallas_call(
        functools.partial(_ffn_ple_kernel, final),
        out_shape=jax.ShapeDtypeStruct(x.shape, F32),
        grid=(n // tm,),
        in_specs=[row_spec,
                  pl.BlockSpec((None, tm, D_PLE), lambda t: (layer, t, 0)),
                  _param_spec(layer, (1, D_MODEL), 1),
                  _param_spec(layer, (D_MODEL, D_FF), 1),
                  _param_spec(layer, (D_MODEL, D_FF), 1),
                  _param_spec(layer, (D_FF, D_MODEL), 1),
                  _param_spec(layer, (1, D_MODEL), 1),
                  _param_spec(layer, (D_MODEL, D_MODEL), 1),
                  _param_spec(layer, (D_PLE, D_MODEL), 1),
                  pl.BlockSpec((1, D_MODEL), lambda t: (0, 0), pipeline_mode=pl.Buffered(1))],
        out_specs=row_spec,
        compiler_params=_compiler_params(1),
        name="ffn2_ple",
    )(x, p, w['ffn2_norm'], w['ffn2_w_gate'], w['ffn2_w_up'], w['ffn2_w_down'],
      w['ple_norm'], w['w_ple_gate'], w['w_ple_proj'], w['final_norm'])


def _mixer_call(x, hist_a, hist_b, layer, w, seqs_per_tile, rows_per_seq):
    groups, t_total, _ = x.shape
    r_total = seqs_per_tile * rows_per_seq
    n_seq = groups * seqs_per_tile
    n_tiles = t_total // r_total
    ps = functools.partial(_param_spec, layer, grid_rank=2)
    in_specs = [
        pl.BlockSpec((None, r_total, D_MODEL), lambda b, t: (b, t, 0)),
        pl.BlockSpec((seqs_per_tile, HIST_A_PAD, W_BR), lambda b, t: (b, 0, 0)),
        pl.BlockSpec((seqs_per_tile, HIST_B_PAD, W_BR), lambda b, t: (b, 0, 0)),
        ps((1, D_MODEL)), ps((D_MODEL, COL_END)), ps((1, COL_END - COL_G)),
        ps((CONV_A, W_BR)), ps((1, W_BR)), ps((1, W_BR)), ps((1, W_BR)), ps((W_BR, D_MODEL)),
        ps((CONV_B, W_BR)), ps((W_BR, D_MODEL)),
        ps((1, W_BR)), ps((1, W_BR)), ps((H_C, MLP_CHUNK, MLP_CHUNK)), ps((MLP_CHUNK, H_C)),
        ps((W_BR, D_MODEL)), ps((D_MODEL, D_MODEL)),
    ]
    out_shape = (jax.ShapeDtypeStruct(x.shape, F32),
                 jax.ShapeDtypeStruct((n_seq, HIST_A_PAD, W_BR), F32),
                 jax.ShapeDtypeStruct((n_seq, HIST_B_PAD, W_BR), F32),
                 jax.ShapeDtypeStruct((groups, t_total, W_BR), F32))
    out_specs = (pl.BlockSpec((None, r_total, D_MODEL), lambda b, t: (b, t, 0)),
                 pl.BlockSpec((seqs_per_tile, HIST_A_PAD, W_BR), lambda b, t: (b, 0, 0)),
                 pl.BlockSpec((seqs_per_tile, HIST_B_PAD, W_BR), lambda b, t: (b, 0, 0)),
                 pl.BlockSpec((None, r_total, W_BR), lambda b, t: (b, t, 0)))
    return pl.pallas_call(
        functools.partial(_mixer_kernel, seqs_per_tile, rows_per_seq),
        out_shape=out_shape,
        grid=(groups, n_tiles),
        in_specs=in_specs,
        out_specs=out_specs,
        scratch_shapes=[pltpu.VMEM((seqs_per_tile, HIST_A_PAD + rows_per_seq, W_BR), F32),
                        pltpu.VMEM((seqs_per_tile, HIST_B_PAD + rows_per_seq, W_BR), F32)],
        compiler_params=_compiler_params(2),
        name="mixer",
    )(x, hist_a, hist_b, w['mix_norm'], w['w_in'], w['gate_bias'],
      w['conv_a_w'], w['conv_a_b'], w['ln_a_g'], w['ln_a_b'], w['w_a_out'],
      w['conv_b_w'], w['w_b_out'], w['ln_c_g'], w['ln_c_b'], w['w_spatial'], w['b_spatial'],
      w['w_c_out'], w['w_o'])


def _trunk(x, p, hist_a, hist_b, w, depth, seqs_per_tile, rows_per_seq, tm):
    groups, t_total, _ = x.shape
    states_a, states_b, chunk_v = [], [], []
    for i in range(depth):
        x = _ffn_call(x.reshape(groups * t_total, D_MODEL), i, w, tm).reshape(x.shape)
        x, na, nb, vn = _mixer_call(x, hist_a[i], hist_b[i], i, w, seqs_per_tile, rows_per_seq)
        x = _ffn_ple_call(x.reshape(groups * t_total, D_MODEL), p, i, w, tm,
                          final=(i == depth - 1)).reshape(x.shape)
        states_a.append(na[:, HIST_A_PAD - (CONV_A - 1):])
        states_b.append(nb[:, HIST_B_PAD - (CONV_B - 1):])
        chunk_v.append(vn)
    return x, jnp.stack(states_a), jnp.stack(states_b), jnp.stack(chunk_v)


def _front_pad(hist, pad_to):
    return jnp.pad(hist, ((0, 0), (0, 0), (pad_to - hist.shape[2], 0), (0, 0)))


def kernel(x_prompt, x_sample, p_prompt, p_sample, cache_conv_a, cache_conv_b, ffn1_norm, ffn1_w_gate, ffn1_w_up, ffn1_w_down, mix_norm, w_in, gate_bias, conv_a_w, conv_a_b, ln_a_g, ln_a_b, w_a_out, conv_b_w, w_b_out, ln_c_g, ln_c_b, w_spatial, b_spatial, w_c_out, w_o, ffn2_norm, ffn2_w_gate, ffn2_w_up, ffn2_w_down, ple_norm, w_ple_gate, w_ple_proj, final_norm):
    depth = w_in.shape[0]
    batch, seq, _ = x_prompt.shape
    dec_batch, dec_seq, _ = x_sample.shape
    row = lambda a: a.reshape(a.shape[0], 1, a.shape[1])
    w = dict(
        ffn1_norm=row(ffn1_norm), ffn1_w_gate=ffn1_w_gate.astype(BF16), ffn1_w_up=ffn1_w_up.astype(BF16),
        ffn1_w_down=ffn1_w_down.astype(BF16),
        mix_norm=row(mix_norm), w_in=w_in.astype(BF16), gate_bias=row(gate_bias),
        conv_a_w=conv_a_w, conv_a_b=row(conv_a_b), ln_a_g=row(ln_a_g), ln_a_b=row(ln_a_b),
        w_a_out=w_a_out.astype(BF16), conv_b_w=conv_b_w, w_b_out=w_b_out.astype(BF16),
        ln_c_g=row(ln_c_g), ln_c_b=row(ln_c_b), w_spatial=w_spatial,
        b_spatial=jnp.swapaxes(b_spatial, 1, 2), w_c_out=w_c_out.astype(BF16), w_o=w_o.astype(BF16),
        ffn2_norm=row(ffn2_norm), ffn2_w_gate=ffn2_w_gate.astype(BF16), ffn2_w_up=ffn2_w_up.astype(BF16),
        ffn2_w_down=ffn2_w_down.astype(BF16),
        ple_norm=row(ple_norm), w_ple_gate=w_ple_gate.astype(BF16), w_ple_proj=w_ple_proj.astype(BF16),
        final_norm=final_norm.reshape(1, D_MODEL))

    zero_a = jnp.zeros((depth, batch, HIST_A_PAD, W_BR), F32)
    zero_b = jnp.zeros((depth, batch, HIST_B_PAD, W_BR), F32)
    y_prompt, conv_a_prompt, conv_b_prompt, _ = _trunk(
        x_prompt, p_prompt.reshape(depth, batch * seq, D_PLE), zero_a, zero_b, w, depth,
        seqs_per_tile=1, rows_per_seq=256, tm=512)

    y_sample, conv_a_sample, conv_b_sample, chunk_v = _trunk(
        x_sample.reshape(1, dec_batch * dec_seq, D_MODEL),
        p_sample.reshape(depth, dec_batch * dec_seq, D_PLE),
        _front_pad(cache_conv_a, HIST_A_PAD), _front_pad(cache_conv_b, HIST_B_PAD), w, depth,
        seqs_per_tile=dec_batch, rows_per_seq=dec_seq, tm=dec_batch * dec_seq)

    return (y_prompt, y_sample.reshape(x_sample.shape), conv_a_prompt, conv_a_sample,
            conv_b_prompt, conv_b_sample,
            chunk_v.reshape(depth, dec_batch, dec_seq, W_BR))
```

```python
import functools

import jax
import jax.numpy as jnp
from jax import lax
from jax.experimental import pallas as pl
from jax.experimental.pallas import tpu as pltpu

D_MODEL = 1024
D_FF = 2816
D_PLE = 256
W_BR = 512
CONV_A = 31
CONV_B = 3
H_C = 4
C_HEAD = W_BR // H_C
MLP_CHUNK = 128
EPS = 1e-6
COL_A, COL_B, COL_C, COL_G, COL_END = 0, 1024, 2560, 3584, 6656

HIST_A_PAD = 32
HIST_B_PAD = 8
SUBLANES = 8
LANES = 128
CONV_ACC_VREGS = 16
VMEM_LIMIT_BYTES = 56 * 1024 * 1024

BF16 = jnp.bfloat16
F32 = jnp.float32


def _dot(a, b):
    return jnp.dot(a, b, preferred_element_type=F32)


def _rmsnorm(x, g):
    return x * lax.rsqrt(jnp.mean(x * x, axis=-1, keepdims=True) + EPS) * g


def _layernorm(x, g, b):
    mu = jnp.mean(x, axis=-1, keepdims=True)
    xc = x - mu
    return xc * lax.rsqrt(jnp.mean(xc * xc, axis=-1, keepdims=True) + EPS) * g + b


def _swiglu_half_step(x, g, wg_ref, wu_ref, wd_ref):
    h = _rmsnorm(x, g).astype(BF16)
    act = jax.nn.silu(_dot(h, wg_ref[...])) * _dot(h, wu_ref[...])
    return x + 0.5 * _dot(act.astype(BF16), wd_ref[...])


def _ffn_kernel(x_ref, g_ref, wg_ref, wu_ref, wd_ref, o_ref):
    o_ref[...] = _swiglu_half_step(x_ref[...], g_ref[...], wg_ref, wu_ref, wd_ref)


def _ffn_ple_kernel(final, x_ref, p_ref, g_ref, wg_ref, wu_ref, wd_ref,
                    pg_ref, wpg_ref, wpp_ref, fg_ref, o_ref):
    x = _swiglu_half_step(x_ref[...], g_ref[...], wg_ref, wu_ref, wd_ref)
    gate = jax.nn.sigmoid(_dot(_rmsnorm(x, pg_ref[...]).astype(BF16), wpg_ref[...]))
    x = x + gate * _dot(p_ref[...].astype(BF16), wpp_ref[...])
    if final:
        x = _rmsnorm(x, fg_ref[...])
    o_ref[...] = x


def _causal_dwconv(zp_ref, zsh_ref, w_ref, n_taps, first_row, n_seq, n_rows):
    if zsh_ref is not None:
        span = zsh_ref.shape[2]
        for r in range(1, SUBLANES):
            zsh_ref[r - 1] = zp_ref[:, pl.ds(r, span), :]
    rows = min(n_rows, 32)
    seqs = max(1, min(n_seq, CONV_ACC_VREGS * SUBLANES * LANES // (rows * W_BR)))
    w = w_ref[...]
    seq_parts = []
    for s0 in range(0, n_seq, seqs):
        row_parts = []
        for r0 in range(0, n_rows, rows):
            acc = jnp.zeros((seqs, rows, W_BR), F32)
            for k in range(n_taps):
                shift, base = (first_row + k) % SUBLANES, (first_row + k) // SUBLANES * SUBLANES
                if zsh_ref is None:
                    win = zp_ref[pl.ds(s0, seqs), pl.ds(first_row + r0 + k, rows), :]
                elif shift == 0:
                    win = zp_ref[pl.ds(s0, seqs), pl.ds(base + r0, rows), :]
                else:
                    win = zsh_ref[shift - 1, pl.ds(s0, seqs), pl.ds(base + r0, rows), :]
                acc = acc + w[k] * win
            row_parts.append(acc)
        seq_parts.append(row_parts[0] if len(row_parts) == 1 else jnp.concatenate(row_parts, axis=1))
    return seq_parts[0] if len(seq_parts) == 1 else jnp.concatenate(seq_parts, axis=0)


def _spatial_mix(vn, wsp_ref, bsp_ref, n_seq, n_rows):
    t = min(n_rows, MLP_CHUNK)
    r_total = n_seq * n_rows
    tril = lax.broadcasted_iota(jnp.int32, (t, t), 0) >= lax.broadcasted_iota(jnp.int32, (t, t), 1)
    heads = []
    if t == MLP_CHUNK:
        n_chunks = r_total // t
        bias = bsp_ref[...]
        for h in range(H_C):
            cols = slice(h * C_HEAD, (h + 1) * C_HEAD)
            wm = jnp.where(tril, wsp_ref[h], 0.0).astype(BF16)
            rhs = jnp.concatenate([vn[j * t:(j + 1) * t, cols] for j in range(n_chunks)], axis=1)
            mh = _dot(wm, rhs) + bias[:, h:h + 1]
            heads.append(jnp.concatenate([mh[:, j * C_HEAD:(j + 1) * C_HEAD] for j in range(n_chunks)], axis=0))
    else:
        sel = (lax.broadcasted_iota(jnp.int32, (r_total, t), 0) % t
               == lax.broadcasted_iota(jnp.int32, (r_total, t), 1)).astype(F32)
        sel_t = (lax.broadcasted_iota(jnp.int32, (t, r_total), 1) % t
                 == lax.broadcasted_iota(jnp.int32, (t, r_total), 0)).astype(F32)
        same_seq = (lax.broadcasted_iota(jnp.int32, (r_total, r_total), 0) // t
                    == lax.broadcasted_iota(jnp.int32, (r_total, r_total), 1) // t)
        bias = jnp.concatenate([bsp_ref[0:t, :]] * n_seq, axis=0)
        for h in range(H_C):
            cols = slice(h * C_HEAD, (h + 1) * C_HEAD)
            wm = jnp.where(tril, wsp_ref[h, 0:t, 0:t], 0.0)
            big = jnp.where(same_seq, _dot(_dot(sel, wm), sel_t), 0.0).astype(BF16)
            heads.append(_dot(big, vn[:, cols]) + bias[:, h:h + 1])
    return jnp.concatenate(heads, axis=1)


def _mixer_kernel(n_seq, n_rows, x_ref, hista_ref, histb_ref, g_ref, win_ref, gbias_ref,
                  caw_ref, cab_ref, lnag_ref, lnab_ref, waout_ref, cbw_ref, wbout_ref,
                  lncg_ref, lncb_ref, wsp_ref, bsp_ref, wcout_ref, wo_ref,
                  o_ref, newa_ref, newb_ref, vn_ref, zpa_ref, zpb_ref, zsh_ref, h_ref, proj_ref):
    r_total = n_seq * n_rows

    @pl.when(pl.program_id(1) == 0)
    def _():
        zpa_ref[:, 0:HIST_A_PAD, :] = hista_ref[...]
        zpb_ref[:, 0:HIST_B_PAD, :] = histb_ref[...]

    h_ref[...] = _rmsnorm(x_ref[...], g_ref[...]).astype(BF16)

    proj_ref[:, COL_A:COL_B] = _dot(h_ref[...], win_ref[:, COL_A:COL_B])
    za = proj_ref[:, 0:W_BR] * jax.nn.sigmoid(proj_ref[:, W_BR:COL_B])
    zpa_ref[:, HIST_A_PAD:HIST_A_PAD + n_rows, :] = za.reshape(n_seq, n_rows, W_BR)

    proj_ref[:, COL_B:COL_G] = _dot(h_ref[...], win_ref[:, COL_B:COL_G])
    ca = _causal_dwconv(zpa_ref, zsh_ref, caw_ref, CONV_A, HIST_A_PAD - (CONV_A - 1), n_seq, n_rows)
    ca = ca.reshape(r_total, W_BR) + cab_ref[...]
    ya_in = jax.nn.silu(_layernorm(ca, lnag_ref[...], lnab_ref[...])).astype(BF16)
    tail_a = zpa_ref[:, n_rows:n_rows + HIST_A_PAD, :]
    zpa_ref[:, 0:HIST_A_PAD, :] = tail_a
    newa_ref[...] = tail_a

    proj_ref[:, COL_G:COL_END] = _dot(h_ref[...], win_ref[:, COL_G:COL_END])

    zb = proj_ref[:, COL_B + W_BR:COL_B + 2 * W_BR] * proj_ref[:, COL_B + 2 * W_BR:COL_C]
    zpb_ref[:, HIST_B_PAD:HIST_B_PAD + n_rows, :] = zb.reshape(n_seq, n_rows, W_BR)
    cb = _causal_dwconv(zpb_ref, None, cbw_ref, CONV_B, HIST_B_PAD - (CONV_B - 1), n_seq, n_rows)
    yb_in = (proj_ref[:, COL_B:COL_B + W_BR] * cb.reshape(r_total, W_BR)).astype(BF16)
    tail_b = zpb_ref[:, n_rows:n_rows + HIST_B_PAD, :]
    zpb_ref[:, 0:HIST_B_PAD, :] = tail_b
    newb_ref[...] = tail_b

    vn = _layernorm(jax.nn.gelu(proj_ref[:, COL_C + W_BR:COL_G]), lncg_ref[...], lncb_ref[...])
    vn_ref[...] = vn
    gu = jax.nn.gelu(proj_ref[:, COL_C:COL_C + W_BR])

    ya = _dot(ya_in, waout_ref[...])
    yb = _dot(yb_in, wbout_ref[...])
    mixed = _spatial_mix(vn.astype(BF16), wsp_ref, bsp_ref, n_seq, n_rows)
    yc = _dot((gu * mixed).astype(BF16), wcout_ref[...])

    gates = jax.nn.sigmoid(proj_ref[:, COL_G:COL_END] + gbias_ref[...])
    merged = (gates[:, :D_MODEL] * ya + gates[:, D_MODEL:2 * D_MODEL] * yb
              + gates[:, 2 * D_MODEL:] * yc)
    o_ref[...] = x_ref[...] + _dot(merged.astype(BF16), wo_ref[...])


def _param_spec(layer, tail_shape, grid_rank):
    zeros = (0,) * len(tail_shape)
    if grid_rank == 1:
        index_map = lambda t: (layer,) + zeros
    else:
        index_map = lambda b, t: (layer,) + zeros
    return pl.BlockSpec((None,) + tuple(tail_shape), index_map, pipeline_mode=pl.Buffered(1))


def _compiler_params(grid_rank):
    return pltpu.CompilerParams(dimension_semantics=("arbitrary",) * grid_rank,
                                vmem_limit_bytes=VMEM_LIMIT_BYTES)


def _ffn_call(x, layer, w, tm):
    n = x.shape[0]
    row_spec = pl.BlockSpec((tm, D_MODEL), lambda t: (t, 0))
    return pl.pallas_call(
        _ffn_kernel,
        out_shape=jax.ShapeDtypeStruct(x.shape, F32),
        grid=(n // tm,),
        in_specs=[row_spec,
                  _param_spec(layer, (1, D_MODEL), 1),
                  _param_spec(layer, (D_MODEL, D_FF), 1),
                  _param_spec(layer, (D_MODEL, D_FF), 1),
                  _param_spec(layer, (D_FF, D_MODEL), 1)],
        out_specs=row_spec,
        compiler_params=_compiler_params(1),
        name="ffn1",
    )(x, w['ffn1_norm'], w['ffn1_w_gate'], w['ffn1_w_up'], w['ffn1_w_down'])


def _ffn_ple_call(x, p, layer, w, tm, final):
    n = x.shape[0]
    row_spec = pl.BlockSpec((tm, D_MODEL), lambda t: (t, 0))
    return pl.pallas_call(
        functools.partial(_ffn_ple_kernel, final),
        out_shape=jax.ShapeDtypeStruct(x.shape, F32),
        grid=(n // tm,),
        in_specs=[row_spec,
                  pl.BlockSpec((None, tm, D_PLE), lambda t: (layer, t, 0)),
                  _param_spec(layer, (1, D_MODEL), 1),
                  _param_spec(layer, (D_MODEL, D_FF), 1),
                  _param_spec(layer, (D_MODEL, D_FF), 1),
                  _param_spec(layer, (D_FF, D_MODEL), 1),
                  _param_spec(layer, (1, D_MODEL), 1),
                  _param_spec(layer, (D_MODEL, D_MODEL), 1),
                  _param_spec(layer, (D_PLE, D_MODEL), 1),
                  pl.BlockSpec((1, D_MODEL), lambda t: (0, 0), pipeline_mode=pl.Buffered(1))],
        out_specs=row_spec,
        compiler_params=_compiler_params(1),
        name="ffn2_ple",
    )(x, p, w['ffn2_norm'], w['ffn2_w_gate'], w['ffn2_w_up'], w['ffn2_w_down'],
      w['ple_norm'], w['w_ple_gate'], w['w_ple_proj'], w['final_norm'])


def _mixer_call(x, hist_a, hist_b, layer, w, seqs_per_tile, rows_per_seq):
    groups, t_total, _ = x.shape
    r_total = seqs_per_tile * rows_per_seq
    n_seq = groups * seqs_per_tile
    n_tiles = t_total // r_total
    ps = functools.partial(_param_spec, layer, grid_rank=2)
    in_specs = [
        pl.BlockSpec((None, r_total, D_MODEL), lambda b, t: (b, t, 0)),
        pl.BlockSpec((seqs_per_tile, HIST_A_PAD, W_BR), lambda b, t: (b, 0, 0)),
        pl.BlockSpec((seqs_per_tile, HIST_B_PAD, W_BR), lambda b, t: (b, 0, 0)),
        ps((1, D_MODEL)), ps((D_MODEL, COL_END)), ps((1, COL_END - COL_G)),
        ps((CONV_A, W_BR)), ps((1, W_BR)), ps((1, W_BR)), ps((1, W_BR)), ps((W_BR, D_MODEL)),
        ps((CONV_B, W_BR)), ps((W_BR, D_MODEL)),
        ps((1, W_BR)), ps((1, W_BR)), ps((H_C, MLP_CHUNK, MLP_CHUNK)), ps((MLP_CHUNK, H_C)),
        ps((W_BR, D_MODEL)), ps((D_MODEL, D_MODEL)),
    ]
    out_shape = (jax.ShapeDtypeStruct(x.shape, F32),
                 jax.ShapeDtypeStruct((n_seq, HIST_A_PAD, W_BR), F32),
                 jax.ShapeDtypeStruct((n_seq, HIST_B_PAD, W_BR), F32),
                 jax.ShapeDtypeStruct((groups, t_total, W_BR), F32))
    out_specs = (pl.BlockSpec((None, r_total, D_MODEL), lambda b, t: (b, t, 0)),
                 pl.BlockSpec((seqs_per_tile, HIST_A_PAD, W_BR), lambda b, t: (b, 0, 0)),
                 pl.BlockSpec((seqs_per_tile, HIST_B_PAD, W_BR), lambda b, t: (b, 0, 0)),
                 pl.BlockSpec((None, r_total, W_BR), lambda b, t: (b, t, 0)))
    return pl.pallas_call(
        functools.partial(_mixer_kernel, seqs_per_tile, rows_per_seq),
        out_shape=out_shape,
        grid=(groups, n_tiles),
        in_specs=in_specs,
        out_specs=out_specs,
        scratch_shapes=[pltpu.VMEM((seqs_per_tile, HIST_A_PAD + rows_per_seq, W_BR), F32),
                        pltpu.VMEM((seqs_per_tile, HIST_B_PAD + rows_per_seq, W_BR), F32),
                        pltpu.VMEM((SUBLANES - 1, seqs_per_tile,
                                    HIST_A_PAD - SUBLANES + rows_per_seq, W_BR), F32),
                        pltpu.VMEM((r_total, D_MODEL), BF16),
                        pltpu.VMEM((r_total, COL_END), F32)],
        compiler_params=_compiler_params(2),
        name="mixer",
    )(x, hist_a, hist_b, w['mix_norm'], w['w_in'], w['gate_bias'],
      w['conv_a_w'], w['conv_a_b'], w['ln_a_g'], w['ln_a_b'], w['w_a_out'],
      w['conv_b_w'], w['w_b_out'], w['ln_c_g'], w['ln_c_b'], w['w_spatial'], w['b_spatial'],
      w['w_c_out'], w['w_o'])


def _trunk(x, p, hist_a, hist_b, w, depth, seqs_per_tile, rows_per_seq, tm):
    groups, t_total, _ = x.shape
    states_a, states_b, chunk_v = [], [], []
    for i in range(depth):
        x = _ffn_call(x.reshape(groups * t_total, D_MODEL), i, w, tm).reshape(x.shape)
        x, na, nb, vn = _mixer_call(x, hist_a[i], hist_b[i], i, w, seqs_per_tile, rows_per_seq)
        x = _ffn_ple_call(x.reshape(groups * t_total, D_MODEL), p, i, w, tm,
                          final=(i == depth - 1)).reshape(x.shape)
        states_a.append(na[:, HIST_A_PAD - (CONV_A - 1):])
        states_b.append(nb[:, HIST_B_PAD - (CONV_B - 1):])
        chunk_v.append(vn)
    return x, jnp.stack(states_a), jnp.stack(states_b), jnp.stack(chunk_v)


def _front_pad(hist, pad_to):
    return jnp.pad(hist, ((0, 0), (0, 0), (pad_to - hist.shape[2], 0), (0, 0)))


def kernel(x_prompt, x_sample, p_prompt, p_sample, cache_conv_a, cache_conv_b, ffn1_norm, ffn1_w_gate, ffn1_w_up, ffn1_w_down, mix_norm, w_in, gate_bias, conv_a_w, conv_a_b, ln_a_g, ln_a_b, w_a_out, conv_b_w, w_b_out, ln_c_g, ln_c_b, w_spatial, b_spatial, w_c_out, w_o, ffn2_norm, ffn2_w_gate, ffn2_w_up, ffn2_w_down, ple_norm, w_ple_gate, w_ple_proj, final_norm):
    depth = w_in.shape[0]
    batch, seq, _ = x_prompt.shape
    dec_batch, dec_seq, _ = x_sample.shape
    row = lambda a: a.reshape(a.shape[0], 1, a.shape[1])
    w = dict(
        ffn1_norm=row(ffn1_norm), ffn1_w_gate=ffn1_w_gate.astype(BF16), ffn1_w_up=ffn1_w_up.astype(BF16),
        ffn1_w_down=ffn1_w_down.astype(BF16),
        mix_norm=row(mix_norm), w_in=w_in.astype(BF16), gate_bias=row(gate_bias),
        conv_a_w=conv_a_w, conv_a_b=row(conv_a_b), ln_a_g=row(ln_a_g), ln_a_b=row(ln_a_b),
        w_a_out=w_a_out.astype(BF16), conv_b_w=conv_b_w, w_b_out=w_b_out.astype(BF16),
        ln_c_g=row(ln_c_g), ln_c_b=row(ln_c_b), w_spatial=w_spatial,
        b_spatial=jnp.swapaxes(b_spatial, 1, 2), w_c_out=w_c_out.astype(BF16), w_o=w_o.astype(BF16),
        ffn2_norm=row(ffn2_norm), ffn2_w_gate=ffn2_w_gate.astype(BF16), ffn2_w_up=ffn2_w_up.astype(BF16),
        ffn2_w_down=ffn2_w_down.astype(BF16),
        ple_norm=row(ple_norm), w_ple_gate=w_ple_gate.astype(BF16), w_ple_proj=w_ple_proj.astype(BF16),
        final_norm=final_norm.reshape(1, D_MODEL))

    zero_a = jnp.zeros((depth, batch, HIST_A_PAD, W_BR), F32)
    zero_b = jnp.zeros((depth, batch, HIST_B_PAD, W_BR), F32)
    y_prompt, conv_a_prompt, conv_b_prompt, _ = _trunk(
        x_prompt, p_prompt.reshape(depth, batch * seq, D_PLE), zero_a, zero_b, w, depth,
        seqs_per_tile=1, rows_per_seq=256, tm=512)

    y_sample, conv_a_sample, conv_b_sample, chunk_v = _trunk(
        x_sample.reshape(1, dec_batch * dec_seq, D_MODEL),
        p_sample.reshape(depth, dec_batch * dec_seq, D_PLE),
        _front_pad(cache_conv_a, HIST_A_PAD), _front_pad(cache_conv_b, HIST_B_PAD), w, depth,
        seqs_per_tile=dec_batch, rows_per_seq=dec_seq, tm=dec_batch * dec_seq)

    return (y_prompt, y_sample.reshape(x_sample.shape), conv_a_prompt, conv_a_sample,
            conv_b_prompt, conv_b_sample,
            chunk_v.reshape(depth, dec_batch, dec_seq, W_BR))
```

```python
import functools
from typing import NamedTuple

import jax
import jax.numpy as jnp
from jax import lax
from jax.experimental import pallas as pl
from jax.experimental.pallas import tpu as pltpu

D_MODEL = 1024
D_FF = 2816
D_PLE = 256
W_BR = 512
CONV_A = 31
CONV_B = 3
H_C = 4
C_HEAD = W_BR // H_C
MLP_CHUNK = 128
EPS = 1e-6
COL_A, COL_B, COL_C, COL_G, COL_END = 0, 1024, 2560, 3584, 6656

HIST_A_PAD = 32
HIST_B_PAD = 8
SUBLANES = 8
BF16_ROWS = 16
LANES = 128
CONV_ACC_VREGS = 16
VMEM_LIMIT_BYTES = 56 * 1024 * 1024

BF16 = jnp.bfloat16
F32 = jnp.float32


def _dot(a, b):
    return jnp.dot(a, b, preferred_element_type=F32)


def _rmsnorm(x, g):
    return x * lax.rsqrt(jnp.mean(x * x, axis=-1, keepdims=True) + EPS) * g


def _layernorm(x, g, b):
    mu = jnp.mean(x, axis=-1, keepdims=True)
    xc = x - mu
    return xc * lax.rsqrt(jnp.mean(xc * xc, axis=-1, keepdims=True) + EPS) * g + b


def _swiglu_half_step(x, g, wg_ref, wu_ref, wd_ref):
    h = _rmsnorm(x, g).astype(BF16)
    act = jax.nn.silu(_dot(h, wg_ref[...])) * _dot(h, wu_ref[...])
    return x + 0.5 * _dot(act.astype(BF16), wd_ref[...])


def _ffn_kernel(x_ref, g_ref, wg_ref, wu_ref, wd_ref, o_ref):
    o_ref[...] = _swiglu_half_step(x_ref[...], g_ref[...], wg_ref, wu_ref, wd_ref)


def _ffn_ple_kernel(final, x_ref, p_ref, g_ref, wg_ref, wu_ref, wd_ref,
                    pg_ref, wpg_ref, wpp_ref, fg_ref, o_ref):
    x = _swiglu_half_step(x_ref[...], g_ref[...], wg_ref, wu_ref, wd_ref)
    gate = jax.nn.sigmoid(_dot(_rmsnorm(x, pg_ref[...]).astype(BF16), wpg_ref[...]))
    x = x + gate * _dot(p_ref[...].astype(BF16), wpp_ref[...])
    if final:
        x = _rmsnorm(x, fg_ref[...])
    o_ref[...] = x


def _causal_dwconv(zp_ref, zsh_ref, w_ref, n_taps, first_row, n_seq, n_rows):
    if zsh_ref is not None:
        span = zsh_ref.shape[2]
        for r in range(1, SUBLANES):
            zsh_ref[r - 1] = zp_ref[:, pl.ds(r, span), :]
    rows = min(n_rows, 32)
    seqs = max(1, min(n_seq, CONV_ACC_VREGS * SUBLANES * LANES // (rows * W_BR)))
    w = w_ref[...]
    seq_parts = []
    for s0 in range(0, n_seq, seqs):
        row_parts = []
        for r0 in range(0, n_rows, rows):
            acc = jnp.zeros((seqs, rows, W_BR), F32)
            for k in range(n_taps):
                shift, base = (first_row + k) % SUBLANES, (first_row + k) // SUBLANES * SUBLANES
                if zsh_ref is None:
                    win = zp_ref[pl.ds(s0, seqs), pl.ds(first_row + r0 + k, rows), :]
                elif shift == 0:
                    win = zp_ref[pl.ds(s0, seqs), pl.ds(base + r0, rows), :]
                else:
                    win = zsh_ref[shift - 1, pl.ds(s0, seqs), pl.ds(base + r0, rows), :]
                acc = acc + w[k] * win
            row_parts.append(acc)
        seq_parts.append(row_parts[0] if len(row_parts) == 1 else jnp.concatenate(row_parts, axis=1))
    return seq_parts[0] if len(seq_parts) == 1 else jnp.concatenate(seq_parts, axis=0)


def _spatial_mix(vn, wsp_ref, bsp_ref, n_seq, n_rows):
    t = min(n_rows, MLP_CHUNK)
    r_total = n_seq * n_rows
    tril = lax.broadcasted_iota(jnp.int32, (t, t), 0) >= lax.broadcasted_iota(jnp.int32, (t, t), 1)
    heads = []
    if t == MLP_CHUNK:
        n_chunks = r_total // t
        bias = bsp_ref[...]
        for h in range(H_C):
            cols = slice(h * C_HEAD, (h + 1) * C_HEAD)
            wm = jnp.where(tril, wsp_ref[h], 0.0).astype(BF16)
            rhs = jnp.concatenate([vn[j * t:(j + 1) * t, cols] for j in range(n_chunks)], axis=1)
            mh = _dot(wm, rhs) + bias[:, h:h + 1]
            heads.append(jnp.concatenate([mh[:, j * C_HEAD:(j + 1) * C_HEAD] for j in range(n_chunks)], axis=0))
    else:
        sel = (lax.broadcasted_iota(jnp.int32, (r_total, t), 0) % t
               == lax.broadcasted_iota(jnp.int32, (r_total, t), 1)).astype(F32)
        sel_t = (lax.broadcasted_iota(jnp.int32, (t, r_total), 1) % t
                 == lax.broadcasted_iota(jnp.int32, (t, r_total), 0)).astype(F32)
        same_seq = (lax.broadcasted_iota(jnp.int32, (r_total, r_total), 0) // t
                    == lax.broadcasted_iota(jnp.int32, (r_total, r_total), 1) // t)
        bias = jnp.concatenate([bsp_ref[0:t, :]] * n_seq, axis=0)
        for h in range(H_C):
            cols = slice(h * C_HEAD, (h + 1) * C_HEAD)
            wm = jnp.where(tril, wsp_ref[h, 0:t, 0:t], 0.0)
            big = jnp.where(same_seq, _dot(_dot(sel, wm), sel_t), 0.0).astype(BF16)
            heads.append(_dot(big, vn[:, cols]) + bias[:, h:h + 1])
    return jnp.concatenate(heads, axis=1)


def _mixer_kernel(n_seq, n_rows, emit_v, x_ref, hista_ref, histb_ref, g_ref, win_ref, gbias_ref,
                  caw_ref, cab_ref, lnag_ref, lnab_ref, waout_ref, cbw_ref, wbout_ref,
                  lncg_ref, lncb_ref, wsp_ref, bsp_ref, wcout_ref, wo_ref,
                  o_ref, newa_ref, newb_ref, *rest):
    vn_ref = rest[0] if emit_v else None
    zpa_ref, zpb_ref, zsh_ref, h_ref, proj_ref = rest[1 if emit_v else 0:]
    r_total = n_seq * n_rows

    @pl.when(pl.program_id(1) == 0)
    def _():
        zpa_ref[:, 0:HIST_A_PAD, :] = hista_ref[...]
        zpb_ref[:, 0:HIST_B_PAD, :] = histb_ref[...]

    h_ref[...] = _rmsnorm(x_ref[...], g_ref[...]).astype(BF16)

    proj_ref[:, COL_A:COL_B] = _dot(h_ref[...], win_ref[:, COL_A:COL_B])
    za = proj_ref[:, 0:W_BR] * jax.nn.sigmoid(proj_ref[:, W_BR:COL_B])
    zpa_ref[:, HIST_A_PAD:HIST_A_PAD + n_rows, :] = za.reshape(n_seq, n_rows, W_BR)

    proj_ref[:, COL_B:COL_G] = _dot(h_ref[...], win_ref[:, COL_B:COL_G])
    ca = _causal_dwconv(zpa_ref, zsh_ref, caw_ref, CONV_A, HIST_A_PAD - (CONV_A - 1), n_seq, n_rows)
    ca = ca.reshape(r_total, W_BR) + cab_ref[...]
    ya_in = jax.nn.silu(_layernorm(ca, lnag_ref[...], lnab_ref[...])).astype(BF16)
    tail_a = zpa_ref[:, n_rows:n_rows + HIST_A_PAD, :]
    zpa_ref[:, 0:HIST_A_PAD, :] = tail_a
    newa_ref[...] = tail_a

    proj_ref[:, COL_G:COL_END] = _dot(h_ref[...], win_ref[:, COL_G:COL_END])

    zb = proj_ref[:, COL_B + W_BR:COL_B + 2 * W_BR] * proj_ref[:, COL_B + 2 * W_BR:COL_C]
    zpb_ref[:, HIST_B_PAD:HIST_B_PAD + n_rows, :] = zb.reshape(n_seq, n_rows, W_BR)
    cb = _causal_dwconv(zpb_ref, None, cbw_ref, CONV_B, HIST_B_PAD - (CONV_B - 1), n_seq, n_rows)
    yb_in = (proj_ref[:, COL_B:COL_B + W_BR] * cb.reshape(r_total, W_BR)).astype(BF16)
    tail_b = zpb_ref[:, n_rows:n_rows + HIST_B_PAD, :]
    zpb_ref[:, 0:HIST_B_PAD, :] = tail_b
    newb_ref[...] = tail_b

    vn = _layernorm(jax.nn.gelu(proj_ref[:, COL_C + W_BR:COL_G]), lncg_ref[...], lncb_ref[...])
    if emit_v:
        vn_ref[...] = vn
    gu = jax.nn.gelu(proj_ref[:, COL_C:COL_C + W_BR])

    ya = _dot(ya_in, waout_ref[...])
    yb = _dot(yb_in, wbout_ref[...])
    mixed = _spatial_mix(vn.astype(BF16), wsp_ref, bsp_ref, n_seq, n_rows)
    yc = _dot((gu * mixed).astype(BF16), wcout_ref[...])

    gates = jax.nn.sigmoid(proj_ref[:, COL_G:COL_END] + gbias_ref[...])
    merged = (gates[:, :D_MODEL] * ya + gates[:, D_MODEL:2 * D_MODEL] * yb
              + gates[:, 2 * D_MODEL:] * yc)
    o_ref[...] = x_ref[...] + _dot(merged.astype(BF16), wo_ref[...])


class _Cast(NamedTuple):
    src: jax.Array
    layer: int


def _cast_chunks(rows, n_steps):
    units = rows // BF16_ROWS
    n_chunks = max(d for d in range(1, min(units, n_steps) + 1) if units % d == 0)
    return rows // n_chunks, n_chunks


def _with_casts(body, n_in, n_out, n_casts, *refs):
    ins, cast_ins = refs[:n_in], refs[n_in:n_in + n_casts]
    outs = refs[n_in + n_casts:n_in + n_casts + n_out]
    cast_outs = refs[n_in + n_casts + n_out:n_in + 2 * n_casts + n_out]
    body(*ins, *outs, *refs[n_in + 2 * n_casts + n_out:])
    for src_ref, dst_ref in zip(cast_ins, cast_outs):
        dst_ref[...] = src_ref[...].astype(BF16)


def _fused_call(body, name, grid, operands, in_specs, out_shapes, out_specs, scratch, casts):
    n_steps = 1
    for g in grid:
        n_steps *= g
    linear = (lambda t: t) if len(grid) == 1 else (lambda b, t: b * grid[1] + t)
    c_in, c_out, c_shapes = [], [], []
    for c in casts:
        _, rows, cols = c.src.shape
        chunk, n_chunks = _cast_chunks(rows, n_steps)
        at = lambda *g, n=n_chunks: jnp.minimum(linear(*g), n - 1)
        c_in.append(pl.BlockSpec((None, chunk, cols), lambda *g, at=at, layer=c.layer: (layer, at(*g), 0)))
        c_out.append(pl.BlockSpec((chunk, cols), lambda *g, at=at: (at(*g), 0)))
        c_shapes.append(jax.ShapeDtypeStruct((rows, cols), BF16))
    outs = pl.pallas_call(
        functools.partial(_with_casts, body, len(operands), len(out_shapes), len(casts)),
        out_shape=tuple(out_shapes) + tuple(c_shapes),
        grid=grid,
        in_specs=list(in_specs) + c_in,
        out_specs=tuple(out_specs) + tuple(c_out),
        scratch_shapes=scratch,
        compiler_params=pltpu.CompilerParams(dimension_semantics=("arbitrary",) * len(grid),
                                             vmem_limit_bytes=VMEM_LIMIT_BYTES),
        name=name,
    )(*operands, *[c.src for c in casts])
    return outs[:len(out_shapes)], outs[len(out_shapes):]


def _layer_spec(layer, tail_shape):
    zeros = (0,) * len(tail_shape)
    return pl.BlockSpec((None,) + tuple(tail_shape), lambda *g: (layer,) + zeros,
                        pipeline_mode=pl.Buffered(1))


def _resident_spec(shape):
    return pl.BlockSpec(tuple(shape), lambda *g: (0, 0), pipeline_mode=pl.Buffered(1))


def _ffn_call(x, layer, w, wb, tm, casts):
    row_spec = pl.BlockSpec((tm, D_MODEL), lambda t: (t, 0))
    (y,), cast_out = _fused_call(
        _ffn_kernel, "ffn1", (x.shape[0] // tm,),
        [x, w['ffn1_norm'], wb['ffn1_w_gate'], wb['ffn1_w_up'], wb['ffn1_w_down']],
        [row_spec, _layer_spec(layer, (1, D_MODEL)), _resident_spec((D_MODEL, D_FF)),
         _resident_spec((D_MODEL, D_FF)), _resident_spec((D_FF, D_MODEL))],
        [jax.ShapeDtypeStruct(x.shape, F32)], [row_spec], [], casts)
    return y, cast_out


def _ffn_ple_call(x, p, layer, w, wb, tm, final, casts):
    row_spec = pl.BlockSpec((tm, D_MODEL), lambda t: (t, 0))
    (y,), cast_out = _fused_call(
        functools.partial(_ffn_ple_kernel, final), "ffn2_ple", (x.shape[0] // tm,),
        [x, p, w['ffn2_norm'], wb['ffn2_w_gate'], wb['ffn2_w_up'], wb['ffn2_w_down'],
         w['ple_norm'], wb['w_ple_gate'], wb['w_ple_proj'], w['final_norm']],
        [row_spec, pl.BlockSpec((None, tm, D_PLE), lambda t: (layer, t, 0)),
         _layer_spec(layer, (1, D_MODEL)), _resident_spec((D_MODEL, D_FF)),
         _resident_spec((D_MODEL, D_FF)), _resident_spec((D_FF, D_MODEL)),
         _layer_spec(layer, (1, D_MODEL)), _resident_spec((D_MODEL, D_MODEL)),
         _resident_spec((D_PLE, D_MODEL)), _resident_spec((1, D_MODEL))],
        [jax.ShapeDtypeStruct(x.shape, F32)], [row_spec], [], casts)
    return y, cast_out


def _mixer_call(x, hist_a, hist_b, layer, w, wb, seqs_per_tile, rows_per_seq, emit_v, casts):
    groups, t_total, _ = x.shape
    r_total = seqs_per_tile * rows_per_seq
    n_seq = groups * seqs_per_tile
    ls = functools.partial(_layer_spec, layer)
    rows_spec = lambda width: pl.BlockSpec((None, r_total, width), lambda b, t: (b, t, 0))
    state_spec = lambda pad: pl.BlockSpec((seqs_per_tile, pad, W_BR), lambda b, t: (b, 0, 0))
    in_specs = [
        rows_spec(D_MODEL), state_spec(HIST_A_PAD), state_spec(HIST_B_PAD),
        ls((1, D_MODEL)), _resident_spec((D_MODEL, COL_END)), ls((1, COL_END - COL_G)),
        ls((CONV_A, W_BR)), ls((1, W_BR)), ls((1, W_BR)), ls((1, W_BR)), _resident_spec((W_BR, D_MODEL)),
        ls((CONV_B, W_BR)), _resident_spec((W_BR, D_MODEL)),
        ls((1, W_BR)), ls((1, W_BR)), ls((H_C, MLP_CHUNK, MLP_CHUNK)), ls((MLP_CHUNK, H_C)),
        _resident_spec((W_BR, D_MODEL)), _resident_spec((D_MODEL, D_MODEL)),
    ]
    out_shapes = [jax.ShapeDtypeStruct(x.shape, F32),
                  jax.ShapeDtypeStruct((n_seq, HIST_A_PAD, W_BR), F32),
                  jax.ShapeDtypeStruct((n_seq, HIST_B_PAD, W_BR), F32)]
    out_specs = [rows_spec(D_MODEL), state_spec(HIST_A_PAD), state_spec(HIST_B_PAD)]
    if emit_v:
        out_shapes.append(jax.ShapeDtypeStruct((groups, t_total, W_BR), F32))
        out_specs.append(rows_spec(W_BR))
    scratch = [pltpu.VMEM((seqs_per_tile, HIST_A_PAD + rows_per_seq, W_BR), F32),
               pltpu.VMEM((seqs_per_tile, HIST_B_PAD + rows_per_seq, W_BR), F32),
               pltpu.VMEM((SUBLANES - 1, seqs_per_tile, HIST_A_PAD - SUBLANES + rows_per_seq, W_BR), F32),
               pltpu.VMEM((r_total, D_MODEL), BF16),
               pltpu.VMEM((r_total, COL_END), F32)]
    outs, cast_out = _fused_call(
        functools.partial(_mixer_kernel, seqs_per_tile, rows_per_seq, emit_v), "mixer",
        (groups, t_total // r_total),
        [x, hist_a, hist_b, w['mix_norm'], wb['w_in'], w['gate_bias'],
         w['conv_a_w'], w['conv_a_b'], w['ln_a_g'], w['ln_a_b'], wb['w_a_out'],
         w['conv_b_w'], wb['w_b_out'], w['ln_c_g'], w['ln_c_b'], w['w_spatial'], w['b_spatial'],
         wb['w_c_out'], wb['w_o']],
        in_specs, out_shapes, out_specs, scratch, casts)
    return (*outs, None)[:4], cast_out


FFN1_WEIGHTS = ('ffn1_w_gate', 'ffn1_w_up', 'ffn1_w_down')
MIXER_WEIGHTS = ('w_in', 'w_a_out', 'w_b_out', 'w_c_out', 'w_o')
FFN2_WEIGHTS = ('ffn2_w_gate', 'ffn2_w_up', 'ffn2_w_down', 'w_ple_gate', 'w_ple_proj')
CAST_ONLY_STEPS = 16


def _trunk(x, p, hist_a, hist_b, w, wb, depth, seqs_per_tile, rows_per_seq, tm, emit_v, raw=None):
    groups, t_total, _ = x.shape
    casts = lambda names, layer: ([_Cast(raw[n], layer) for n in names]
                                  if raw is not None and layer < depth else [])
    if raw is not None:
        _, done = _fused_call(lambda: None, "cast", (CAST_ONLY_STEPS,), [], [], [], [], [],
                              casts(FFN1_WEIGHTS, 0))
        wb[0].update(zip(FFN1_WEIGHTS, done))
    states_a, states_b, chunk_v = [], [], []
    for i in range(depth):
        x2, done = _ffn_call(x.reshape(groups * t_total, D_MODEL), i, w, wb[i], tm, casts(MIXER_WEIGHTS, i))
        wb[i].update(zip(MIXER_WEIGHTS, done))
        (x3, na, nb, vn), done = _mixer_call(x2.reshape(x.shape), hist_a[i], hist_b[i], i, w, wb[i],
                                             seqs_per_tile, rows_per_seq, emit_v, casts(FFN2_WEIGHTS, i))
        wb[i].update(zip(FFN2_WEIGHTS, done))
        x4, done = _ffn_ple_call(x3.reshape(groups * t_total, D_MODEL), p, i, w, wb[i], tm,
                                 i == depth - 1, casts(FFN1_WEIGHTS, i + 1))
        if done:
            wb[i + 1].update(zip(FFN1_WEIGHTS, done))
        x = x4.reshape(x.shape)
        states_a.append(na[:, HIST_A_PAD - (CONV_A - 1):])
        states_b.append(nb[:, HIST_B_PAD - (CONV_B - 1):])
        chunk_v.append(vn)
    return x, jnp.stack(states_a), jnp.stack(states_b), (jnp.stack(chunk_v) if emit_v else None)


def _front_pad(hist, pad_to):
    return jnp.pad(hist, ((0, 0), (0, 0), (pad_to - hist.shape[2], 0), (0, 0)))


def kernel(x_prompt, x_sample, p_prompt, p_sample, cache_conv_a, cache_conv_b, ffn1_norm, ffn1_w_gate, ffn1_w_up, ffn1_w_down, mix_norm, w_in, gate_bias, conv_a_w, conv_a_b, ln_a_g, ln_a_b, w_a_out, conv_b_w, w_b_out, ln_c_g, ln_c_b, w_spatial, b_spatial, w_c_out, w_o, ffn2_norm, ffn2_w_gate, ffn2_w_up, ffn2_w_down, ple_norm, w_ple_gate, w_ple_proj, final_norm):
    depth = w_in.shape[0]
    batch, seq, _ = x_prompt.shape
    dec_batch, dec_seq, _ = x_sample.shape
    row = lambda a: a.reshape(a.shape[0], 1, a.shape[1])
    w = dict(
        ffn1_norm=row(ffn1_norm), mix_norm=row(mix_norm), gate_bias=row(gate_bias),
        conv_a_w=conv_a_w, conv_a_b=row(conv_a_b), ln_a_g=row(ln_a_g), ln_a_b=row(ln_a_b),
        conv_b_w=conv_b_w, ln_c_g=row(ln_c_g), ln_c_b=row(ln_c_b), w_spatial=w_spatial,
        b_spatial=jnp.swapaxes(b_spatial, 1, 2), ffn2_norm=row(ffn2_norm), ple_norm=row(ple_norm),
        final_norm=final_norm.reshape(1, D_MODEL))
    raw = dict(ffn1_w_gate=ffn1_w_gate, ffn1_w_up=ffn1_w_up, ffn1_w_down=ffn1_w_down,
               w_in=w_in, w_a_out=w_a_out, w_b_out=w_b_out, w_c_out=w_c_out, w_o=w_o,
               ffn2_w_gate=ffn2_w_gate, ffn2_w_up=ffn2_w_up, ffn2_w_down=ffn2_w_down,
               w_ple_gate=w_ple_gate, w_ple_proj=w_ple_proj)
    wb = [dict() for _ in range(depth)]

    zero_a = jnp.zeros((depth, batch, HIST_A_PAD, W_BR), F32)
    zero_b = jnp.zeros((depth, batch, HIST_B_PAD, W_BR), F32)
    y_prompt, conv_a_prompt, conv_b_prompt, _ = _trunk(
        x_prompt, p_prompt.reshape(depth, batch * seq, D_PLE), zero_a, zero_b, w, wb, depth,
        seqs_per_tile=1, rows_per_seq=256, tm=512, emit_v=False, raw=raw)

    y_sample, conv_a_sample, conv_b_sample, chunk_v = _trunk(
        x_sample.reshape(1, dec_batch * dec_seq, D_MODEL),
        p_sample.reshape(depth, dec_batch * dec_seq, D_PLE),
        _front_pad(cache_conv_a, HIST_A_PAD), _front_pad(cache_conv_b, HIST_B_PAD), w, wb, depth,
        seqs_per_tile=dec_batch, rows_per_seq=dec_seq, tm=dec_batch * dec_seq, emit_v=True)

    return (y_prompt, y_sample.reshape(x_sample.shape), conv_a_prompt, conv_a_sample,
            conv_b_prompt, conv_b_sample,
            chunk_v.reshape(depth, dec_batch, dec_seq, W_BR))
```

```python
import functools
from typing import NamedTuple

import jax
import jax.numpy as jnp
from jax import lax
from jax.experimental import pallas as pl
from jax.experimental.pallas import tpu as pltpu

D_MODEL = 1024
D_FF = 2816
D_PLE = 256
W_BR = 512
CONV_A = 31
CONV_B = 3
H_C = 4
C_HEAD = W_BR // H_C
MLP_CHUNK = 128
EPS = 1e-6
COL_A, COL_B, COL_C, COL_G, COL_END = 0, 1024, 2560, 3584, 6656

HIST_A_PAD = 32
HIST_B_PAD = 8
SUBLANES = 8
BF16_ROWS = 16
LANES = 128
CONV_ACC_VREGS = 16
VMEM_LIMIT_BYTES = 56 * 1024 * 1024

BF16 = jnp.bfloat16
F32 = jnp.float32


def _dot(a, b):
    return jnp.dot(a, b, preferred_element_type=F32)


def _rmsnorm(x, g):
    return x * lax.rsqrt(jnp.mean(x * x, axis=-1, keepdims=True) + EPS) * g


def _layernorm(x, g, b):
    mu = jnp.mean(x, axis=-1, keepdims=True)
    xc = x - mu
    return xc * lax.rsqrt(jnp.mean(xc * xc, axis=-1, keepdims=True) + EPS) * g + b


def _swiglu_half_step(x, g, wg_ref, wu_ref, wd_ref):
    h = _rmsnorm(x, g).astype(BF16)
    act = jax.nn.silu(_dot(h, wg_ref[...])) * _dot(h, wu_ref[...])
    return x + 0.5 * _dot(act.astype(BF16), wd_ref[...])


def _two_groups(n_prompt_steps, prompt_body, sample_body):
    step = pl.program_id(0)
    pl.when(step < n_prompt_steps)(prompt_body)
    pl.when(step == n_prompt_steps)(sample_body)


def _ffn_kernel(n_prompt_steps, xp_ref, xs_ref, g_ref, wg_ref, wu_ref, wd_ref, op_ref, os_ref):
    def tile(x_ref, o_ref):
        o_ref[...] = _swiglu_half_step(x_ref[...], g_ref[...], wg_ref, wu_ref, wd_ref)

    _two_groups(n_prompt_steps, functools.partial(tile, xp_ref, op_ref), functools.partial(tile, xs_ref, os_ref))


def _ffn_ple_kernel(final, n_prompt_steps, xp_ref, xs_ref, pp_ref, ps_ref, g_ref, wg_ref, wu_ref, wd_ref,
                    pg_ref, wpg_ref, wpp_ref, fg_ref, op_ref, os_ref):
    def tile(x_ref, p_ref, o_ref):
        x = _swiglu_half_step(x_ref[...], g_ref[...], wg_ref, wu_ref, wd_ref)
        gate = jax.nn.sigmoid(_dot(_rmsnorm(x, pg_ref[...]).astype(BF16), wpg_ref[...]))
        x = x + gate * _dot(p_ref[...].astype(BF16), wpp_ref[...])
        if final:
            x = _rmsnorm(x, fg_ref[...])
        o_ref[...] = x

    _two_groups(n_prompt_steps, functools.partial(tile, xp_ref, pp_ref, op_ref),
                functools.partial(tile, xs_ref, ps_ref, os_ref))


def _causal_dwconv(zp_ref, zsh_ref, w_ref, n_taps, first_row, n_seq, n_rows):
    if zsh_ref is not None:
        span = zsh_ref.shape[2]
        for r in range(1, SUBLANES):
            zsh_ref[r - 1] = zp_ref[:, pl.ds(r, span), :]
    rows = min(n_rows, 32)
    seqs = max(1, min(n_seq, CONV_ACC_VREGS * SUBLANES * LANES // (rows * W_BR)))
    w = w_ref[...]
    seq_parts = []
    for s0 in range(0, n_seq, seqs):
        row_parts = []
        for r0 in range(0, n_rows, rows):
            acc = jnp.zeros((seqs, rows, W_BR), F32)
            for k in range(n_taps):
                shift, base = (first_row + k) % SUBLANES, (first_row + k) // SUBLANES * SUBLANES
                if zsh_ref is None:
                    win = zp_ref[pl.ds(s0, seqs), pl.ds(first_row + r0 + k, rows), :]
                elif shift == 0:
                    win = zp_ref[pl.ds(s0, seqs), pl.ds(base + r0, rows), :]
                else:
                    win = zsh_ref[shift - 1, pl.ds(s0, seqs), pl.ds(base + r0, rows), :]
                acc = acc + w[k] * win
            row_parts.append(acc)
        seq_parts.append(row_parts[0] if len(row_parts) == 1 else jnp.concatenate(row_parts, axis=1))
    return seq_parts[0] if len(seq_parts) == 1 else jnp.concatenate(seq_parts, axis=0)


def _spatial_mix(vn, wsp_ref, bsp_ref, n_seq, n_rows):
    t = min(n_rows, MLP_CHUNK)
    r_total = n_seq * n_rows
    tril = lax.broadcasted_iota(jnp.int32, (t, t), 0) >= lax.broadcasted_iota(jnp.int32, (t, t), 1)
    heads = []
    if t == MLP_CHUNK:
        n_chunks = r_total // t
        bias = bsp_ref[...]
        for h in range(H_C):
            cols = slice(h * C_HEAD, (h + 1) * C_HEAD)
            wm = jnp.where(tril, wsp_ref[h], 0.0).astype(BF16)
            rhs = jnp.concatenate([vn[j * t:(j + 1) * t, cols] for j in range(n_chunks)], axis=1)
            mh = _dot(wm, rhs) + bias[:, h:h + 1]
            heads.append(jnp.concatenate([mh[:, j * C_HEAD:(j + 1) * C_HEAD] for j in range(n_chunks)], axis=0))
    else:
        sel = (lax.broadcasted_iota(jnp.int32, (r_total, t), 0) % t
               == lax.broadcasted_iota(jnp.int32, (r_total, t), 1)).astype(F32)
        sel_t = (lax.broadcasted_iota(jnp.int32, (t, r_total), 1) % t
                 == lax.broadcasted_iota(jnp.int32, (t, r_total), 0)).astype(F32)
        same_seq = (lax.broadcasted_iota(jnp.int32, (r_total, r_total), 0) // t
                    == lax.broadcasted_iota(jnp.int32, (r_total, r_total), 1) // t)
        bias = jnp.concatenate([bsp_ref[0:t, :]] * n_seq, axis=0)
        for h in range(H_C):
            cols = slice(h * C_HEAD, (h + 1) * C_HEAD)
            wm = jnp.where(tril, wsp_ref[h, 0:t, 0:t], 0.0)
            big = jnp.where(same_seq, _dot(_dot(sel, wm), sel_t), 0.0).astype(BF16)
            heads.append(_dot(big, vn[:, cols]) + bias[:, h:h + 1])
    return jnp.concatenate(heads, axis=1)


def _mixer_tile(n_seq, n_rows, fresh, x_ref, hista_ref, histb_ref, params, o_ref, newa_ref, newb_ref,
                vn_ref, zpa_ref, zpb_ref, zsh_ref, h_ref, proj_ref):
    (g_ref, win_ref, gbias_ref, caw_ref, cab_ref, lnag_ref, lnab_ref, waout_ref, cbw_ref, wbout_ref,
     lncg_ref, lncb_ref, wsp_ref, bsp_ref, wcout_ref, wo_ref) = params
    r_total = n_seq * n_rows

    @pl.when(fresh)
    def _():
        zpa_ref[:, 0:HIST_A_PAD, :] = hista_ref[...]
        zpb_ref[:, 0:HIST_B_PAD, :] = histb_ref[...]

    h_ref[...] = _rmsnorm(x_ref[...], g_ref[...]).astype(BF16)

    proj_ref[:, COL_A:COL_B] = _dot(h_ref[...], win_ref[:, COL_A:COL_B])
    za = proj_ref[:, 0:W_BR] * jax.nn.sigmoid(proj_ref[:, W_BR:COL_B])
    zpa_ref[:, HIST_A_PAD:HIST_A_PAD + n_rows, :] = za.reshape(n_seq, n_rows, W_BR)

    proj_ref[:, COL_B:COL_G] = _dot(h_ref[...], win_ref[:, COL_B:COL_G])
    ca = _causal_dwconv(zpa_ref, zsh_ref, caw_ref, CONV_A, HIST_A_PAD - (CONV_A - 1), n_seq, n_rows)
    ca = ca.reshape(r_total, W_BR) + cab_ref[...]
    ya_in = jax.nn.silu(_layernorm(ca, lnag_ref[...], lnab_ref[...])).astype(BF16)
    tail_a = zpa_ref[:, n_rows:n_rows + HIST_A_PAD, :]
    zpa_ref[:, 0:HIST_A_PAD, :] = tail_a
    newa_ref[...] = tail_a

    proj_ref[:, COL_G:COL_END] = _dot(h_ref[...], win_ref[:, COL_G:COL_END])

    zb = proj_ref[:, COL_B + W_BR:COL_B + 2 * W_BR] * proj_ref[:, COL_B + 2 * W_BR:COL_C]
    zpb_ref[:, HIST_B_PAD:HIST_B_PAD + n_rows, :] = zb.reshape(n_seq, n_rows, W_BR)
    cb = _causal_dwconv(zpb_ref, None, cbw_ref, CONV_B, HIST_B_PAD - (CONV_B - 1), n_seq, n_rows)
    yb_in = (proj_ref[:, COL_B:COL_B + W_BR] * cb.reshape(r_total, W_BR)).astype(BF16)
    tail_b = zpb_ref[:, n_rows:n_rows + HIST_B_PAD, :]
    zpb_ref[:, 0:HIST_B_PAD, :] = tail_b
    newb_ref[...] = tail_b

    vn = _layernorm(jax.nn.gelu(proj_ref[:, COL_C + W_BR:COL_G]), lncg_ref[...], lncb_ref[...])
    if vn_ref is not None:
        vn_ref[...] = vn
    gu = jax.nn.gelu(proj_ref[:, COL_C:COL_C + W_BR])

    ya = _dot(ya_in, waout_ref[...])
    yb = _dot(yb_in, wbout_ref[...])
    mixed = _spatial_mix(vn.astype(BF16), wsp_ref, bsp_ref, n_seq, n_rows)
    yc = _dot((gu * mixed).astype(BF16), wcout_ref[...])

    gates = jax.nn.sigmoid(proj_ref[:, COL_G:COL_END] + gbias_ref[...])
    merged = (gates[:, :D_MODEL] * ya + gates[:, D_MODEL:2 * D_MODEL] * yb
              + gates[:, 2 * D_MODEL:] * yc)
    o_ref[...] = x_ref[...] + _dot(merged.astype(BF16), wo_ref[...])


N_MIXER_PARAMS = 16


def _mixer_kernel(n_prompt_steps, tiles_per_seq, prompt_rows, dec_batch, dec_seq,
                  xp_ref, xs_ref, hap_ref, has_ref, hbp_ref, hbs_ref, *refs):
    params = refs[:N_MIXER_PARAMS]
    (op_ref, os_ref, nap_ref, nas_ref, nbp_ref, nbs_ref, vn_ref,
     zpa_p, zpb_p, zsh_p, zpa_s, zpb_s, zsh_s, h_ref, proj_ref) = refs[N_MIXER_PARAMS:]
    step = pl.program_id(0)
    _two_groups(
        n_prompt_steps,
        lambda: _mixer_tile(1, prompt_rows, step % tiles_per_seq == 0, xp_ref, hap_ref, hbp_ref, params,
                            op_ref, nap_ref, nbp_ref, None, zpa_p, zpb_p, zsh_p, h_ref, proj_ref),
        lambda: _mixer_tile(dec_batch, dec_seq, step == n_prompt_steps, xs_ref, has_ref, hbs_ref, params,
                            os_ref, nas_ref, nbs_ref, vn_ref, zpa_s, zpb_s, zsh_s, h_ref, proj_ref))


class _Cast(NamedTuple):
    src: jax.Array
    layer: int


def _cast_chunks(rows, n_steps):
    units = rows // BF16_ROWS
    n_chunks = max(d for d in range(1, min(units, n_steps) + 1) if units % d == 0)
    return rows // n_chunks, n_chunks


def _with_casts(body, n_in, n_out, n_casts, *refs):
    ins, cast_ins = refs[:n_in], refs[n_in:n_in + n_casts]
    outs = refs[n_in + n_casts:n_in + n_casts + n_out]
    cast_outs = refs[n_in + n_casts + n_out:n_in + 2 * n_casts + n_out]
    body(*ins, *outs, *refs[n_in + 2 * n_casts + n_out:])
    for src_ref, dst_ref in zip(cast_ins, cast_outs):
        dst_ref[...] = src_ref[...].astype(BF16)


def _fused_call(body, name, grid, operands, in_specs, out_shapes, out_specs, scratch, casts):
    n_steps = 1
    for g in grid:
        n_steps *= g
    linear = (lambda t: t) if len(grid) == 1 else (lambda b, t: b * grid[1] + t)
    c_in, c_out, c_shapes = [], [], []
    for c in casts:
        _, rows, cols = c.src.shape
        chunk, n_chunks = _cast_chunks(rows, n_steps)
        at = lambda *g, n=n_chunks: jnp.minimum(linear(*g), n - 1)
        c_in.append(pl.BlockSpec((None, chunk, cols), lambda *g, at=at, layer=c.layer: (layer, at(*g), 0)))
        c_out.append(pl.BlockSpec((chunk, cols), lambda *g, at=at: (at(*g), 0)))
        c_shapes.append(jax.ShapeDtypeStruct((rows, cols), BF16))
    outs = pl.pallas_call(
        functools.partial(_with_casts, body, len(operands), len(out_shapes), len(casts)),
        out_shape=tuple(out_shapes) + tuple(c_shapes),
        grid=grid,
        in_specs=list(in_specs) + c_in,
        out_specs=tuple(out_specs) + tuple(c_out),
        scratch_shapes=scratch,
        compiler_params=pltpu.CompilerParams(dimension_semantics=("arbitrary",) * len(grid),
                                             vmem_limit_bytes=VMEM_LIMIT_BYTES),
        name=name,
    )(*operands, *[c.src for c in casts])
    return outs[:len(out_shapes)], outs[len(out_shapes):]


def _layer_spec(layer, tail_shape):
    zeros = (0,) * len(tail_shape)
    return pl.BlockSpec((None,) + tuple(tail_shape), lambda *g: (layer,) + zeros,
                        pipeline_mode=pl.Buffered(1))


def _resident_spec(shape):
    return pl.BlockSpec(tuple(shape), lambda *g: (0, 0), pipeline_mode=pl.Buffered(1))


def _tile_spec(rows, width, n_prompt_steps):
    return pl.BlockSpec((rows, width), lambda t: (jnp.minimum(t, n_prompt_steps - 1), 0))


def _whole_spec(shape):
    return pl.BlockSpec(tuple(shape), lambda t: (0,) * len(shape))


def _ffn_call(xp, xs, layer, w, wb, tm, casts):
    n_p = xp.shape[0] // tm
    outs, cast_out = _fused_call(
        functools.partial(_ffn_kernel, n_p), "ffn1", (n_p + 1,),
        [xp, xs, w['ffn1_norm'], wb['ffn1_w_gate'], wb['ffn1_w_up'], wb['ffn1_w_down']],
        [_tile_spec(tm, D_MODEL, n_p), _whole_spec(xs.shape), _layer_spec(layer, (1, D_MODEL)),
         _resident_spec((D_MODEL, D_FF)), _resident_spec((D_MODEL, D_FF)), _resident_spec((D_FF, D_MODEL))],
        [jax.ShapeDtypeStruct(xp.shape, F32), jax.ShapeDtypeStruct(xs.shape, F32)],
        [_tile_spec(tm, D_MODEL, n_p), _whole_spec(xs.shape)], [], casts)
    return outs, cast_out


def _ffn_ple_call(xp, xs, pp, ps, layer, w, wb, tm, final, casts):
    n_p = xp.shape[0] // tm
    outs, cast_out = _fused_call(
        functools.partial(_ffn_ple_kernel, final, n_p), "ffn2_ple", (n_p + 1,),
        [xp, xs, pp, ps, w['ffn2_norm'], wb['ffn2_w_gate'], wb['ffn2_w_up'], wb['ffn2_w_down'],
         w['ple_norm'], wb['w_ple_gate'], wb['w_ple_proj'], w['final_norm']],
        [_tile_spec(tm, D_MODEL, n_p), _whole_spec(xs.shape),
         pl.BlockSpec((None, tm, D_PLE), lambda t: (layer, jnp.minimum(t, n_p - 1), 0)),
         pl.BlockSpec((None,) + ps.shape[1:], lambda t: (layer, 0, 0)),
         _layer_spec(layer, (1, D_MODEL)), _resident_spec((D_MODEL, D_FF)),
         _resident_spec((D_MODEL, D_FF)), _resident_spec((D_FF, D_MODEL)),
         _layer_spec(layer, (1, D_MODEL)), _resident_spec((D_MODEL, D_MODEL)),
         _resident_spec((D_PLE, D_MODEL)), _resident_spec((1, D_MODEL))],
        [jax.ShapeDtypeStruct(xp.shape, F32), jax.ShapeDtypeStruct(xs.shape, F32)],
        [_tile_spec(tm, D_MODEL, n_p), _whole_spec(xs.shape)], [], casts)
    return outs, cast_out


def _mixer_call(xp, xs, hist_ap, hist_as, hist_bp, hist_bs, layer, w, wb, seq, dec_seq, tile_rows, casts):
    n_p = xp.shape[0] // tile_rows
    tiles_per_seq = seq // tile_rows
    dec_batch = xs.shape[0] // dec_seq
    assert xs.shape[0] == tile_rows
    ls = functools.partial(_layer_spec, layer)
    seq_of_step = lambda t: jnp.minimum(t, n_p - 1) // tiles_per_seq
    state_p = lambda pad: pl.BlockSpec((1, pad, W_BR), lambda t: (seq_of_step(t), 0, 0))
    state_s = lambda pad: _whole_spec((dec_batch, pad, W_BR))
    in_specs = [
        _tile_spec(tile_rows, D_MODEL, n_p), _whole_spec(xs.shape),
        state_p(HIST_A_PAD), state_s(HIST_A_PAD), state_p(HIST_B_PAD), state_s(HIST_B_PAD),
        ls((1, D_MODEL)), _resident_spec((D_MODEL, COL_END)), ls((1, COL_END - COL_G)),
        ls((CONV_A, W_BR)), ls((1, W_BR)), ls((1, W_BR)), ls((1, W_BR)), _resident_spec((W_BR, D_MODEL)),
        ls((CONV_B, W_BR)), _resident_spec((W_BR, D_MODEL)),
        ls((1, W_BR)), ls((1, W_BR)), ls((H_C, MLP_CHUNK, MLP_CHUNK)), ls((MLP_CHUNK, H_C)),
        _resident_spec((W_BR, D_MODEL)), _resident_spec((D_MODEL, D_MODEL)),
    ]
    f32 = lambda *shape: jax.ShapeDtypeStruct(shape, F32)
    out_shapes = [f32(*xp.shape), f32(*xs.shape),
                  f32(hist_ap.shape[0], HIST_A_PAD, W_BR), f32(dec_batch, HIST_A_PAD, W_BR),
                  f32(hist_bp.shape[0], HIST_B_PAD, W_BR), f32(dec_batch, HIST_B_PAD, W_BR),
                  f32(xs.shape[0], W_BR)]
    out_specs = [_tile_spec(tile_rows, D_MODEL, n_p), _whole_spec(xs.shape),
                 state_p(HIST_A_PAD), state_s(HIST_A_PAD), state_p(HIST_B_PAD), state_s(HIST_B_PAD),
                 _whole_spec((xs.shape[0], W_BR))]

    def conv_scratch(n_seq, n_rows):
        return [pltpu.VMEM((n_seq, HIST_A_PAD + n_rows, W_BR), F32),
                pltpu.VMEM((n_seq, HIST_B_PAD + n_rows, W_BR), F32),
                pltpu.VMEM((SUBLANES - 1, n_seq, HIST_A_PAD - SUBLANES + n_rows, W_BR), F32)]

    scratch = (conv_scratch(1, tile_rows) + conv_scratch(dec_batch, dec_seq)
               + [pltpu.VMEM((tile_rows, D_MODEL), BF16), pltpu.VMEM((tile_rows, COL_END), F32)])
    params = [w['mix_norm'], wb['w_in'], w['gate_bias'],
              w['conv_a_w'], w['conv_a_b'], w['ln_a_g'], w['ln_a_b'], wb['w_a_out'],
              w['conv_b_w'], wb['w_b_out'], w['ln_c_g'], w['ln_c_b'], w['w_spatial'], w['b_spatial'],
              wb['w_c_out'], wb['w_o']]
    assert len(params) == N_MIXER_PARAMS
    return _fused_call(
        functools.partial(_mixer_kernel, n_p, tiles_per_seq, tile_rows, dec_batch, dec_seq), "mixer",
        (n_p + 1,), [xp, xs, hist_ap, hist_as, hist_bp, hist_bs] + params,
        in_specs, out_shapes, out_specs, scratch, casts)


FFN1_WEIGHTS = ('ffn1_w_gate', 'ffn1_w_up', 'ffn1_w_down')
MIXER_WEIGHTS = ('w_in', 'w_a_out', 'w_b_out', 'w_c_out', 'w_o')
FFN2_WEIGHTS = ('ffn2_w_gate', 'ffn2_w_up', 'ffn2_w_down', 'w_ple_gate', 'w_ple_proj')
CAST_ONLY_STEPS = 16


MIXER_TILE_ROWS = 256
FFN_TILE_ROWS = 512


def _trunk(xp, xs, pp, ps, hist_ap, hist_as, hist_bp, hist_bs, w, raw, depth, seq, dec_seq):
    casts = lambda names, layer: [_Cast(raw[n], layer) for n in names] if layer < depth else []
    wb = [dict() for _ in range(depth)]
    _, done = _fused_call(lambda: None, "cast", (CAST_ONLY_STEPS,), [], [], [], [], [], casts(FFN1_WEIGHTS, 0))
    wb[0].update(zip(FFN1_WEIGHTS, done))
    states, chunk_v = [], []
    for i in range(depth):
        (xp, xs), done = _ffn_call(xp, xs, i, w, wb[i], FFN_TILE_ROWS, casts(MIXER_WEIGHTS, i))
        wb[i].update(zip(MIXER_WEIGHTS, done))
        (xp, xs, nap, nas, nbp, nbs, vn), done = _mixer_call(
            xp, xs, hist_ap[i], hist_as[i], hist_bp[i], hist_bs[i], i, w, wb[i], seq, dec_seq,
            MIXER_TILE_ROWS, casts(FFN2_WEIGHTS, i))
        wb[i].update(zip(FFN2_WEIGHTS, done))
        (xp, xs), done = _ffn_ple_call(xp, xs, pp, ps, i, w, wb[i], FFN_TILE_ROWS, i == depth - 1,
                                       casts(FFN1_WEIGHTS, i + 1))
        if done:
            wb[i + 1].update(zip(FFN1_WEIGHTS, done))
        states.append([nap[:, HIST_A_PAD - (CONV_A - 1):], nas[:, HIST_A_PAD - (CONV_A - 1):],
                       nbp[:, HIST_B_PAD - (CONV_B - 1):], nbs[:, HIST_B_PAD - (CONV_B - 1):]])
        chunk_v.append(vn)
    return (xp, xs, *[jnp.stack(s) for s in zip(*states)], jnp.stack(chunk_v))


def _front_pad(hist, pad_to):
    return jnp.pad(hist, ((0, 0), (0, 0), (pad_to - hist.shape[2], 0), (0, 0)))


def kernel(x_prompt, x_sample, p_prompt, p_sample, cache_conv_a, cache_conv_b, ffn1_norm, ffn1_w_gate, ffn1_w_up, ffn1_w_down, mix_norm, w_in, gate_bias, conv_a_w, conv_a_b, ln_a_g, ln_a_b, w_a_out, conv_b_w, w_b_out, ln_c_g, ln_c_b, w_spatial, b_spatial, w_c_out, w_o, ffn2_norm, ffn2_w_gate, ffn2_w_up, ffn2_w_down, ple_norm, w_ple_gate, w_ple_proj, final_norm):
    depth = w_in.shape[0]
    batch, seq, _ = x_prompt.shape
    dec_batch, dec_seq, _ = x_sample.shape
    row = lambda a: a.reshape(a.shape[0], 1, a.shape[1])
    w = dict(
        ffn1_norm=row(ffn1_norm), mix_norm=row(mix_norm), gate_bias=row(gate_bias),
        conv_a_w=conv_a_w, conv_a_b=row(conv_a_b), ln_a_g=row(ln_a_g), ln_a_b=row(ln_a_b),
        conv_b_w=conv_b_w, ln_c_g=row(ln_c_g), ln_c_b=row(ln_c_b), w_spatial=w_spatial,
        b_spatial=jnp.swapaxes(b_spatial, 1, 2), ffn2_norm=row(ffn2_norm), ple_norm=row(ple_norm),
        final_norm=final_norm.reshape(1, D_MODEL))
    raw = dict(ffn1_w_gate=ffn1_w_gate, ffn1_w_up=ffn1_w_up, ffn1_w_down=ffn1_w_down,
               w_in=w_in, w_a_out=w_a_out, w_b_out=w_b_out, w_c_out=w_c_out, w_o=w_o,
               ffn2_w_gate=ffn2_w_gate, ffn2_w_up=ffn2_w_up, ffn2_w_down=ffn2_w_down,
               w_ple_gate=w_ple_gate, w_ple_proj=w_ple_proj)

    (y_prompt, y_sample, conv_a_prompt, conv_a_sample, conv_b_prompt, conv_b_sample, chunk_v) = _trunk(
        x_prompt.reshape(batch * seq, D_MODEL), x_sample.reshape(dec_batch * dec_seq, D_MODEL),
        p_prompt.reshape(depth, batch * seq, D_PLE), p_sample.reshape(depth, dec_batch * dec_seq, D_PLE),
        jnp.zeros((depth, batch, HIST_A_PAD, W_BR), F32), _front_pad(cache_conv_a, HIST_A_PAD),
        jnp.zeros((depth, batch, HIST_B_PAD, W_BR), F32), _front_pad(cache_conv_b, HIST_B_PAD),
        w, raw, depth, seq, dec_seq)

    return (y_prompt.reshape(x_prompt.shape), y_sample.reshape(x_sample.shape),
            conv_a_prompt, conv_a_sample, conv_b_prompt, conv_b_sample,
            chunk_v.reshape(depth, dec_batch, dec_seq, W_BR))
```

```python
import functools
from typing import NamedTuple

import jax
import jax.numpy as jnp
from jax import lax
from jax.experimental import pallas as pl
from jax.experimental.pallas import tpu as pltpu

D_MODEL = 1024
D_FF = 2816
D_PLE = 256
W_BR = 512
CONV_A = 31
CONV_B = 3
H_C = 4
C_HEAD = W_BR // H_C
MLP_CHUNK = 128
EPS = 1e-6
COL_A, COL_B, COL_C, COL_G, COL_END = 0, 1024, 2560, 3584, 6656

HIST_A_PAD = 32
HIST_B_PAD = 8
SUBLANES = 8
BF16_ROWS = 16
LANES = 128
CONV_ACC_VREGS = 16
VMEM_LIMIT_BYTES = 62 * 1024 * 1024

BF16 = jnp.bfloat16
F32 = jnp.float32


def _dot(a, b):
    return jnp.dot(a, b, preferred_element_type=F32)


def _rmsnorm(x, g):
    return x * lax.rsqrt(jnp.mean(x * x, axis=-1, keepdims=True) + EPS) * g


def _layernorm(x, g, b):
    mu = jnp.mean(x, axis=-1, keepdims=True)
    xc = x - mu
    return xc * lax.rsqrt(jnp.mean(xc * xc, axis=-1, keepdims=True) + EPS) * g + b


def _swiglu_half_step(x, g, wg_ref, wu_ref, wd_ref):
    h = _rmsnorm(x, g).astype(BF16)
    act = jax.nn.silu(_dot(h, wg_ref[...])) * _dot(h, wu_ref[...])
    return x + 0.5 * _dot(act.astype(BF16), wd_ref[...])


def _two_groups(n_prompt_steps, prompt_body, sample_body):
    step = pl.program_id(0)
    pl.when(step < n_prompt_steps)(prompt_body)
    pl.when(step == n_prompt_steps)(sample_body)


def _ffn_kernel(n_prompt_steps, xp_ref, xs_ref, g_ref, wg_ref, wu_ref, wd_ref, op_ref, os_ref):
    def tile(x_ref, o_ref):
        o_ref[...] = _swiglu_half_step(x_ref[...], g_ref[...], wg_ref, wu_ref, wd_ref)

    _two_groups(n_prompt_steps, functools.partial(tile, xp_ref, op_ref), functools.partial(tile, xs_ref, os_ref))


def _ffn_ple_kernel(final, n_prompt_steps, xp_ref, xs_ref, pp_ref, ps_ref, g_ref, wg_ref, wu_ref, wd_ref,
                    pg_ref, wpg_ref, wpp_ref, fg_ref, op_ref, os_ref):
    def tile(x_ref, p_ref, o_ref):
        x = _swiglu_half_step(x_ref[...], g_ref[...], wg_ref, wu_ref, wd_ref)
        gate = jax.nn.sigmoid(_dot(_rmsnorm(x, pg_ref[...]).astype(BF16), wpg_ref[...]))
        x = x + gate * _dot(p_ref[...].astype(BF16), wpp_ref[...])
        if final:
            x = _rmsnorm(x, fg_ref[...])
        o_ref[...] = x

    _two_groups(n_prompt_steps, functools.partial(tile, xp_ref, pp_ref, op_ref),
                functools.partial(tile, xs_ref, ps_ref, os_ref))


def _causal_dwconv(zp_ref, zsh_ref, w_ref, n_taps, first_row, n_seq, n_rows):
    if zsh_ref is not None:
        span = zsh_ref.shape[2]
        for r in range(1, SUBLANES):
            zsh_ref[r - 1] = zp_ref[:, pl.ds(r, span), :]
    rows = min(n_rows, 32)
    seqs = max(1, min(n_seq, CONV_ACC_VREGS * SUBLANES * LANES // (rows * W_BR)))
    w = w_ref[...]
    seq_parts = []
    for s0 in range(0, n_seq, seqs):
        row_parts = []
        for r0 in range(0, n_rows, rows):
            acc = jnp.zeros((seqs, rows, W_BR), F32)
            for k in range(n_taps):
                shift, base = (first_row + k) % SUBLANES, (first_row + k) // SUBLANES * SUBLANES
                if zsh_ref is None:
                    win = zp_ref[pl.ds(s0, seqs), pl.ds(first_row + r0 + k, rows), :]
                elif shift == 0:
                    win = zp_ref[pl.ds(s0, seqs), pl.ds(base + r0, rows), :]
                else:
                    win = zsh_ref[shift - 1, pl.ds(s0, seqs), pl.ds(base + r0, rows), :]
                acc = acc + w[k] * win
            row_parts.append(acc)
        seq_parts.append(row_parts[0] if len(row_parts) == 1 else jnp.concatenate(row_parts, axis=1))
    return seq_parts[0] if len(seq_parts) == 1 else jnp.concatenate(seq_parts, axis=0)


def _spatial_mix(vn, wsp_ref, bsp_ref, n_seq, n_rows):
    t = min(n_rows, MLP_CHUNK)
    r_total = n_seq * n_rows
    tril = lax.broadcasted_iota(jnp.int32, (t, t), 0) >= lax.broadcasted_iota(jnp.int32, (t, t), 1)
    heads = []
    if t == MLP_CHUNK:
        n_chunks = r_total // t
        bias = bsp_ref[...]
        for h in range(H_C):
            cols = slice(h * C_HEAD, (h + 1) * C_HEAD)
            wm = jnp.where(tril, wsp_ref[h], 0.0).astype(BF16)
            rhs = jnp.concatenate([vn[j * t:(j + 1) * t, cols] for j in range(n_chunks)], axis=1)
            mh = _dot(wm, rhs) + bias[:, h:h + 1]
            heads.append(jnp.concatenate([mh[:, j * C_HEAD:(j + 1) * C_HEAD] for j in range(n_chunks)], axis=0))
    else:
        sel = (lax.broadcasted_iota(jnp.int32, (r_total, t), 0) % t
               == lax.broadcasted_iota(jnp.int32, (r_total, t), 1)).astype(F32)
        sel_t = (lax.broadcasted_iota(jnp.int32, (t, r_total), 1) % t
                 == lax.broadcasted_iota(jnp.int32, (t, r_total), 0)).astype(F32)
        same_seq = (lax.broadcasted_iota(jnp.int32, (r_total, r_total), 0) // t
                    == lax.broadcasted_iota(jnp.int32, (r_total, r_total), 1) // t)
        bias = jnp.concatenate([bsp_ref[0:t, :]] * n_seq, axis=0)
        for h in range(H_C):
            cols = slice(h * C_HEAD, (h + 1) * C_HEAD)
            wm = jnp.where(tril, wsp_ref[h, 0:t, 0:t], 0.0)
            big = jnp.where(same_seq, _dot(_dot(sel, wm), sel_t), 0.0).astype(BF16)
            heads.append(_dot(big, vn[:, cols]) + bias[:, h:h + 1])
    return jnp.concatenate(heads, axis=1)


def _mixer_tile(n_seq, n_rows, fresh, x_ref, hista_ref, histb_ref, params, o_ref, newa_ref, newb_ref,
                vn_ref, zpa_ref, zpb_ref, zsh_ref, h_ref, proj_ref):
    (g_ref, win_ref, gbias_ref, caw_ref, cab_ref, lnag_ref, lnab_ref, waout_ref, cbw_ref, wbout_ref,
     lncg_ref, lncb_ref, wsp_ref, bsp_ref, wcout_ref, wo_ref) = params
    r_total = n_seq * n_rows

    @pl.when(fresh)
    def _():
        zpa_ref[:, 0:HIST_A_PAD, :] = hista_ref[...]
        zpb_ref[:, 0:HIST_B_PAD, :] = histb_ref[...]

    h_ref[...] = _rmsnorm(x_ref[...], g_ref[...]).astype(BF16)

    proj_ref[:, COL_A:COL_B] = _dot(h_ref[...], win_ref[:, COL_A:COL_B])
    za = proj_ref[:, 0:W_BR] * jax.nn.sigmoid(proj_ref[:, W_BR:COL_B])
    zpa_ref[:, HIST_A_PAD:HIST_A_PAD + n_rows, :] = za.reshape(n_seq, n_rows, W_BR)

    proj_ref[:, COL_B:COL_G] = _dot(h_ref[...], win_ref[:, COL_B:COL_G])
    ca = _causal_dwconv(zpa_ref, zsh_ref, caw_ref, CONV_A, HIST_A_PAD - (CONV_A - 1), n_seq, n_rows)
    ca = ca.reshape(r_total, W_BR) + cab_ref[...]
    ya_in = jax.nn.silu(_layernorm(ca, lnag_ref[...], lnab_ref[...])).astype(BF16)
    tail_a = zpa_ref[:, n_rows:n_rows + HIST_A_PAD, :]
    zpa_ref[:, 0:HIST_A_PAD, :] = tail_a
    newa_ref[...] = tail_a

    proj_ref[:, COL_G:COL_END] = _dot(h_ref[...], win_ref[:, COL_G:COL_END])

    zb = proj_ref[:, COL_B + W_BR:COL_B + 2 * W_BR] * proj_ref[:, COL_B + 2 * W_BR:COL_C]
    zpb_ref[:, HIST_B_PAD:HIST_B_PAD + n_rows, :] = zb.reshape(n_seq, n_rows, W_BR)
    cb = _causal_dwconv(zpb_ref, None, cbw_ref, CONV_B, HIST_B_PAD - (CONV_B - 1), n_seq, n_rows)
    yb_in = (proj_ref[:, COL_B:COL_B + W_BR] * cb.reshape(r_total, W_BR)).astype(BF16)
    tail_b = zpb_ref[:, n_rows:n_rows + HIST_B_PAD, :]
    zpb_ref[:, 0:HIST_B_PAD, :] = tail_b
    newb_ref[...] = tail_b

    vn = _layernorm(jax.nn.gelu(proj_ref[:, COL_C + W_BR:COL_G]), lncg_ref[...], lncb_ref[...])
    if vn_ref is not None:
        vn_ref[...] = vn
    gu = jax.nn.gelu(proj_ref[:, COL_C:COL_C + W_BR])

    ya = _dot(ya_in, waout_ref[...])
    yb = _dot(yb_in, wbout_ref[...])
    mixed = _spatial_mix(vn.astype(BF16), wsp_ref, bsp_ref, n_seq, n_rows)
    yc = _dot((gu * mixed).astype(BF16), wcout_ref[...])

    gates = jax.nn.sigmoid(proj_ref[:, COL_G:COL_END] + gbias_ref[...])
    merged = (gates[:, :D_MODEL] * ya + gates[:, D_MODEL:2 * D_MODEL] * yb
              + gates[:, 2 * D_MODEL:] * yc)
    o_ref[...] = x_ref[...] + _dot(merged.astype(BF16), wo_ref[...])


N_MIXER_PARAMS = 16


def _mixer_kernel(n_prompt_steps, tiles_per_seq, prompt_rows, dec_batch, dec_seq,
                  xp_ref, xs_ref, hap_ref, has_ref, hbp_ref, hbs_ref, *refs):
    params = refs[:N_MIXER_PARAMS]
    (op_ref, os_ref, nap_ref, nas_ref, nbp_ref, nbs_ref, vn_ref,
     zpa_p, zpb_p, zsh_p, zpa_s, zpb_s, zsh_s, h_ref, proj_ref) = refs[N_MIXER_PARAMS:]
    step = pl.program_id(0)
    _two_groups(
        n_prompt_steps,
        lambda: _mixer_tile(1, prompt_rows, step % tiles_per_seq == 0, xp_ref, hap_ref, hbp_ref, params,
                            op_ref, nap_ref, nbp_ref, None, zpa_p, zpb_p, zsh_p, h_ref, proj_ref),
        lambda: _mixer_tile(dec_batch, dec_seq, step == n_prompt_steps, xs_ref, has_ref, hbs_ref, params,
                            os_ref, nas_ref, nbs_ref, vn_ref, zpa_s, zpb_s, zsh_s, h_ref, proj_ref))


class _Cast(NamedTuple):
    src: jax.Array
    layer: int


def _cast_chunks(rows, n_steps):
    units = rows // BF16_ROWS
    n_chunks = max(d for d in range(1, min(units, n_steps) + 1) if units % d == 0)
    return rows // n_chunks, n_chunks


def _with_casts(body, n_in, n_out, n_casts, *refs):
    ins, cast_ins = refs[:n_in], refs[n_in:n_in + n_casts]
    outs = refs[n_in + n_casts:n_in + n_casts + n_out]
    cast_outs = refs[n_in + n_casts + n_out:n_in + 2 * n_casts + n_out]
    body(*ins, *outs, *refs[n_in + 2 * n_casts + n_out:])
    for src_ref, dst_ref in zip(cast_ins, cast_outs):
        dst_ref[...] = src_ref[...].astype(BF16)


def _fused_call(body, name, grid, operands, in_specs, out_shapes, out_specs, scratch, casts):
    n_steps = 1
    for g in grid:
        n_steps *= g
    linear = (lambda t: t) if len(grid) == 1 else (lambda b, t: b * grid[1] + t)
    c_in, c_out, c_shapes = [], [], []
    for c in casts:
        _, rows, cols = c.src.shape
        chunk, n_chunks = _cast_chunks(rows, n_steps)
        at = lambda *g, n=n_chunks: jnp.minimum(linear(*g), n - 1)
        c_in.append(pl.BlockSpec((None, chunk, cols), lambda *g, at=at, layer=c.layer: (layer, at(*g), 0)))
        c_out.append(pl.BlockSpec((chunk, cols), lambda *g, at=at: (at(*g), 0)))
        c_shapes.append(jax.ShapeDtypeStruct((rows, cols), BF16))
    outs = pl.pallas_call(
        functools.partial(_with_casts, body, len(operands), len(out_shapes), len(casts)),
        out_shape=tuple(out_shapes) + tuple(c_shapes),
        grid=grid,
        in_specs=list(in_specs) + c_in,
        out_specs=tuple(out_specs) + tuple(c_out),
        scratch_shapes=scratch,
        compiler_params=pltpu.CompilerParams(dimension_semantics=("arbitrary",) * len(grid),
                                             vmem_limit_bytes=VMEM_LIMIT_BYTES),
        name=name,
    )(*operands, *[c.src for c in casts])
    return outs[:len(out_shapes)], outs[len(out_shapes):]


def _layer_spec(layer, tail_shape):
    zeros = (0,) * len(tail_shape)
    return pl.BlockSpec((None,) + tuple(tail_shape), lambda *g: (layer,) + zeros,
                        pipeline_mode=pl.Buffered(1))


def _resident_spec(shape):
    return pl.BlockSpec(tuple(shape), lambda *g: (0, 0), pipeline_mode=pl.Buffered(1))


def _tile_spec(rows, width, n_prompt_steps):
    return pl.BlockSpec((rows, width), lambda t: (jnp.minimum(t, n_prompt_steps - 1), 0))


def _whole_spec(shape):
    return pl.BlockSpec(tuple(shape), lambda t: (0,) * len(shape))


def _ffn_call(xp, xs, layer, w, wb, tm, casts):
    n_p = xp.shape[0] // tm
    outs, cast_out = _fused_call(
        functools.partial(_ffn_kernel, n_p), "ffn1", (n_p + 1,),
        [xp, xs, w['ffn1_norm'], wb['ffn1_w_gate'], wb['ffn1_w_up'], wb['ffn1_w_down']],
        [_tile_spec(tm, D_MODEL, n_p), _whole_spec(xs.shape), _layer_spec(layer, (1, D_MODEL)),
         _resident_spec((D_MODEL, D_FF)), _resident_spec((D_MODEL, D_FF)), _resident_spec((D_FF, D_MODEL))],
        [jax.ShapeDtypeStruct(xp.shape, F32), jax.ShapeDtypeStruct(xs.shape, F32)],
        [_tile_spec(tm, D_MODEL, n_p), _whole_spec(xs.shape)], [], casts)
    return outs, cast_out


def _ffn_ple_call(xp, xs, pp, ps, layer, w, wb, tm, final, casts):
    n_p = xp.shape[0] // tm
    outs, cast_out = _fused_call(
        functools.partial(_ffn_ple_kernel, final, n_p), "ffn2_ple", (n_p + 1,),
        [xp, xs, pp, ps, w['ffn2_norm'], wb['ffn2_w_gate'], wb['ffn2_w_up'], wb['ffn2_w_down'],
         w['ple_norm'], wb['w_ple_gate'], wb['w_ple_proj'], w['final_norm']],
        [_tile_spec(tm, D_MODEL, n_p), _whole_spec(xs.shape),
         pl.BlockSpec((None, tm, D_PLE), lambda t: (layer, jnp.minimum(t, n_p - 1), 0)),
         pl.BlockSpec((None,) + ps.shape[1:], lambda t: (layer, 0, 0)),
         _layer_spec(layer, (1, D_MODEL)), _resident_spec((D_MODEL, D_FF)),
         _resident_spec((D_MODEL, D_FF)), _resident_spec((D_FF, D_MODEL)),
         _layer_spec(layer, (1, D_MODEL)), _resident_spec((D_MODEL, D_MODEL)),
         _resident_spec((D_PLE, D_MODEL)), _resident_spec((1, D_MODEL))],
        [jax.ShapeDtypeStruct(xp.shape, F32), jax.ShapeDtypeStruct(xs.shape, F32)],
        [_tile_spec(tm, D_MODEL, n_p), _whole_spec(xs.shape)], [], casts)
    return outs, cast_out


def _mixer_call(xp, xs, hist_ap, hist_as, hist_bp, hist_bs, layer, w, wb, seq, dec_seq, tile_rows, casts):
    n_p = xp.shape[0] // tile_rows
    tiles_per_seq = seq // tile_rows
    dec_batch = xs.shape[0] // dec_seq
    assert xs.shape[0] == tile_rows
    ls = functools.partial(_layer_spec, layer)
    seq_of_step = lambda t: jnp.minimum(t, n_p - 1) // tiles_per_seq
    state_p = lambda pad: pl.BlockSpec((1, pad, W_BR), lambda t: (seq_of_step(t), 0, 0))
    state_s = lambda pad: _whole_spec((dec_batch, pad, W_BR))
    in_specs = [
        _tile_spec(tile_rows, D_MODEL, n_p), _whole_spec(xs.shape),
        state_p(HIST_A_PAD), state_s(HIST_A_PAD), state_p(HIST_B_PAD), state_s(HIST_B_PAD),
        ls((1, D_MODEL)), _resident_spec((D_MODEL, COL_END)), ls((1, COL_END - COL_G)),
        ls((CONV_A, W_BR)), ls((1, W_BR)), ls((1, W_BR)), ls((1, W_BR)), _resident_spec((W_BR, D_MODEL)),
        ls((CONV_B, W_BR)), _resident_spec((W_BR, D_MODEL)),
        ls((1, W_BR)), ls((1, W_BR)), ls((H_C, MLP_CHUNK, MLP_CHUNK)), ls((MLP_CHUNK, H_C)),
        _resident_spec((W_BR, D_MODEL)), _resident_spec((D_MODEL, D_MODEL)),
    ]
    f32 = lambda *shape: jax.ShapeDtypeStruct(shape, F32)
    out_shapes = [f32(*xp.shape), f32(*xs.shape),
                  f32(hist_ap.shape[0], HIST_A_PAD, W_BR), f32(dec_batch, HIST_A_PAD, W_BR),
                  f32(hist_bp.shape[0], HIST_B_PAD, W_BR), f32(dec_batch, HIST_B_PAD, W_BR),
                  f32(xs.shape[0], W_BR)]
    out_specs = [_tile_spec(tile_rows, D_MODEL, n_p), _whole_spec(xs.shape),
                 state_p(HIST_A_PAD), state_s(HIST_A_PAD), state_p(HIST_B_PAD), state_s(HIST_B_PAD),
                 _whole_spec((xs.shape[0], W_BR))]

    def conv_scratch(n_seq, n_rows):
        return [pltpu.VMEM((n_seq, HIST_A_PAD + n_rows, W_BR), F32),
                pltpu.VMEM((n_seq, HIST_B_PAD + n_rows, W_BR), F32),
                pltpu.VMEM((SUBLANES - 1, n_seq, HIST_A_PAD - SUBLANES + n_rows, W_BR), F32)]

    scratch = (conv_scratch(1, tile_rows) + conv_scratch(dec_batch, dec_seq)
               + [pltpu.VMEM((tile_rows, D_MODEL), BF16), pltpu.VMEM((tile_rows, COL_END), F32)])
    params = [w['mix_norm'], wb['w_in'], w['gate_bias'],
              w['conv_a_w'], w['conv_a_b'], w['ln_a_g'], w['ln_a_b'], wb['w_a_out'],
              w['conv_b_w'], wb['w_b_out'], w['ln_c_g'], w['ln_c_b'], w['w_spatial'], w['b_spatial'],
              wb['w_c_out'], wb['w_o']]
    assert len(params) == N_MIXER_PARAMS
    return _fused_call(
        functools.partial(_mixer_kernel, n_p, tiles_per_seq, tile_rows, dec_batch, dec_seq), "mixer",
        (n_p + 1,), [xp, xs, hist_ap, hist_as, hist_bp, hist_bs] + params,
        in_specs, out_shapes, out_specs, scratch, casts)


FFN1_WEIGHTS = ('ffn1_w_gate', 'ffn1_w_up', 'ffn1_w_down')
MIXER_WEIGHTS = ('w_in', 'w_a_out', 'w_b_out', 'w_c_out', 'w_o')
FFN2_WEIGHTS = ('ffn2_w_gate', 'ffn2_w_up', 'ffn2_w_down', 'w_ple_gate', 'w_ple_proj')
CAST_ONLY_STEPS = 16


MIXER_TILE_ROWS = 256
FFN_TILE_ROWS = 1024


def _trunk(xp, xs, pp, ps, hist_ap, hist_as, hist_bp, hist_bs, w, raw, depth, seq, dec_seq):
    casts = lambda names, layer: [_Cast(raw[n], layer) for n in names] if layer < depth else []
    wb = [dict() for _ in range(depth)]
    _, done = _fused_call(lambda: None, "cast", (CAST_ONLY_STEPS,), [], [], [], [], [], casts(FFN1_WEIGHTS, 0))
    wb[0].update(zip(FFN1_WEIGHTS, done))
    states, chunk_v = [], []
    for i in range(depth):
        (xp, xs), done = _ffn_call(xp, xs, i, w, wb[i], FFN_TILE_ROWS, casts(MIXER_WEIGHTS, i))
        wb[i].update(zip(MIXER_WEIGHTS, done))
        (xp, xs, nap, nas, nbp, nbs, vn), done = _mixer_call(
            xp, xs, hist_ap[i], hist_as[i], hist_bp[i], hist_bs[i], i, w, wb[i], seq, dec_seq,
            MIXER_TILE_ROWS, casts(FFN2_WEIGHTS, i))
        wb[i].update(zip(FFN2_WEIGHTS, done))
        (xp, xs), done = _ffn_ple_call(xp, xs, pp, ps, i, w, wb[i], FFN_TILE_ROWS, i == depth - 1,
                                       casts(FFN1_WEIGHTS, i + 1))
        if done:
            wb[i + 1].update(zip(FFN1_WEIGHTS, done))
        states.append([nap[:, HIST_A_PAD - (CONV_A - 1):], nas[:, HIST_A_PAD - (CONV_A - 1):],
                       nbp[:, HIST_B_PAD - (CONV_B - 1):], nbs[:, HIST_B_PAD - (CONV_B - 1):]])
        chunk_v.append(vn)
    return (xp, xs, *[jnp.stack(s) for s in zip(*states)], jnp.stack(chunk_v))


def _front_pad(hist, pad_to):
    return jnp.pad(hist, ((0, 0), (0, 0), (pad_to - hist.shape[2], 0), (0, 0)))


def kernel(x_prompt, x_sample, p_prompt, p_sample, cache_conv_a, cache_conv_b, ffn1_norm, ffn1_w_gate, ffn1_w_up, ffn1_w_down, mix_norm, w_in, gate_bias, conv_a_w, conv_a_b, ln_a_g, ln_a_b, w_a_out, conv_b_w, w_b_out, ln_c_g, ln_c_b, w_spatial, b_spatial, w_c_out, w_o, ffn2_norm, ffn2_w_gate, ffn2_w_up, ffn2_w_down, ple_norm, w_ple_gate, w_ple_proj, final_norm):
    depth = w_in.shape[0]
    batch, seq, _ = x_prompt.shape
    dec_batch, dec_seq, _ = x_sample.shape
    row = lambda a: a.reshape(a.shape[0], 1, a.shape[1])
    w = dict(
        ffn1_norm=row(ffn1_norm), mix_norm=row(mix_norm), gate_bias=row(gate_bias),
        conv_a_w=conv_a_w, conv_a_b=row(conv_a_b), ln_a_g=row(ln_a_g), ln_a_b=row(ln_a_b),
        conv_b_w=conv_b_w, ln_c_g=row(ln_c_g), ln_c_b=row(ln_c_b), w_spatial=w_spatial,
        b_spatial=jnp.swapaxes(b_spatial, 1, 2), ffn2_norm=row(ffn2_norm), ple_norm=row(ple_norm),
        final_norm=final_norm.reshape(1, D_MODEL))
    raw = dict(ffn1_w_gate=ffn1_w_gate, ffn1_w_up=ffn1_w_up, ffn1_w_down=ffn1_w_down,
               w_in=w_in, w_a_out=w_a_out, w_b_out=w_b_out, w_c_out=w_c_out, w_o=w_o,
               ffn2_w_gate=ffn2_w_gate, ffn2_w_up=ffn2_w_up, ffn2_w_down=ffn2_w_down,
               w_ple_gate=w_ple_gate, w_ple_proj=w_ple_proj)

    (y_prompt, y_sample, conv_a_prompt, conv_a_sample, conv_b_prompt, conv_b_sample, chunk_v) = _trunk(
        x_prompt.reshape(batch * seq, D_MODEL), x_sample.reshape(dec_batch * dec_seq, D_MODEL),
        p_prompt.reshape(depth, batch * seq, D_PLE), p_sample.reshape(depth, dec_batch * dec_seq, D_PLE),
        jnp.zeros((depth, batch, HIST_A_PAD, W_BR), F32), _front_pad(cache_conv_a, HIST_A_PAD),
        jnp.zeros((depth, batch, HIST_B_PAD, W_BR), F32), _front_pad(cache_conv_b, HIST_B_PAD),
        w, raw, depth, seq, dec_seq)

    return (y_prompt.reshape(x_prompt.shape), y_sample.reshape(x_sample.shape),
            conv_a_prompt, conv_a_sample, conv_b_prompt, conv_b_sample,
            chunk_v.reshape(depth, dec_batch, dec_seq, W_BR))
```

```python
import functools
from typing import NamedTuple

import jax
import jax.numpy as jnp
from jax import lax
from jax.experimental import pallas as pl
from jax.experimental.pallas import tpu as pltpu

D_MODEL = 1024
D_FF = 2816
D_PLE = 256
W_BR = 512
CONV_A = 31
CONV_B = 3
H_C = 4
C_HEAD = W_BR // H_C
MLP_CHUNK = 128
EPS = 1e-6
COL_A, COL_B, COL_C, COL_G, COL_END = 0, 1024, 2560, 3584, 6656

HIST_A_PAD = 32
HIST_B_PAD = 8
SUBLANES = 8
BF16_ROWS = 16
LANES = 128
CONV_STRIP_ROWS = 128
PROJ_CHUNK = 512
CHUNKS_WITH_CONV = 9
GRID_BOUND = 1 << 30
CONV_ACC_VREGS = 16
VMEM_LIMIT_BYTES = 56 * 1024 * 1024

BF16 = jnp.bfloat16
F32 = jnp.float32


def _dot(a, b):
    return jnp.dot(a, b, preferred_element_type=F32)


def _rmsnorm(x, g):
    return x * lax.rsqrt(jnp.mean(x * x, axis=-1, keepdims=True) + EPS) * g


def _layernorm(x, g, b):
    mu = jnp.mean(x, axis=-1, keepdims=True)
    xc = x - mu
    return xc * lax.rsqrt(jnp.mean(xc * xc, axis=-1, keepdims=True) + EPS) * g + b


def _swiglu_half_step(x, g, wg_ref, wu_ref, wd_ref):
    h = _rmsnorm(x, g).astype(BF16)
    act = jax.nn.silu(_dot(h, wg_ref[...])) * _dot(h, wu_ref[...])
    return x + 0.5 * _dot(act.astype(BF16), wd_ref[...])


def _two_groups(n_prompt_steps, prompt_body, sample_body):
    step = pl.program_id(0)
    pl.when(step < n_prompt_steps)(prompt_body)
    pl.when(step == n_prompt_steps)(sample_body)


def _ffn_kernel(n_prompt_steps, xp_ref, xs_ref, g_ref, wg_ref, wu_ref, wd_ref, op_ref, os_ref):
    def tile(x_ref, o_ref):
        o_ref[...] = _swiglu_half_step(x_ref[...], g_ref[...], wg_ref, wu_ref, wd_ref)

    _two_groups(n_prompt_steps, functools.partial(tile, xp_ref, op_ref), functools.partial(tile, xs_ref, os_ref))


def _ffn_ple_kernel(final, n_prompt_steps, xp_ref, xs_ref, pp_ref, ps_ref, g_ref, wg_ref, wu_ref, wd_ref,
                    pg_ref, wpg_ref, wpp_ref, fg_ref, op_ref, os_ref):
    def tile(x_ref, p_ref, o_ref):
        x = _swiglu_half_step(x_ref[...], g_ref[...], wg_ref, wu_ref, wd_ref)
        gate = jax.nn.sigmoid(_dot(_rmsnorm(x, pg_ref[...]).astype(BF16), wpg_ref[...]))
        x = x + gate * _dot(p_ref[...].astype(BF16), wpp_ref[...])
        if final:
            x = _rmsnorm(x, fg_ref[...])
        o_ref[...] = x

    _two_groups(n_prompt_steps, functools.partial(tile, xp_ref, pp_ref, op_ref),
                functools.partial(tile, xs_ref, ps_ref, os_ref))


def _causal_dwconv(zp_ref, zsh_ref, w_ref, n_taps, first_row, n_seq, n_rows):
    if zsh_ref is not None:
        span = zsh_ref.shape[2]
        for r in range(1, SUBLANES):
            zsh_ref[r - 1] = zp_ref[:, pl.ds(r, span), :]
    rows = min(n_rows, 32)
    seqs = max(1, min(n_seq, CONV_ACC_VREGS * SUBLANES * LANES // (rows * W_BR)))
    w = w_ref[...]
    seq_parts = []
    for s0 in range(0, n_seq, seqs):
        row_parts = []
        for r0 in range(0, n_rows, rows):
            acc = jnp.zeros((seqs, rows, W_BR), F32)
            for k in range(n_taps):
                shift, base = (first_row + k) % SUBLANES, (first_row + k) // SUBLANES * SUBLANES
                if zsh_ref is None:
                    win = zp_ref[pl.ds(s0, seqs), pl.ds(first_row + r0 + k, rows), :]
                elif shift == 0:
                    win = zp_ref[pl.ds(s0, seqs), pl.ds(base + r0, rows), :]
                else:
                    win = zsh_ref[shift - 1, pl.ds(s0, seqs), pl.ds(base + r0, rows), :]
                acc = acc + w[k] * win
            row_parts.append(acc)
        seq_parts.append(row_parts[0] if len(row_parts) == 1 else jnp.concatenate(row_parts, axis=1))
    return seq_parts[0] if len(seq_parts) == 1 else jnp.concatenate(seq_parts, axis=0)


def _conv_a_strip(zp_ref, zsh_ref, w_ref, bias_ref, out_ref, r0, c0, rows):
    first_row = HIST_A_PAD - (CONV_A - 1)
    lanes = pl.ds(c0, LANES)
    groups = {}
    for k in range(CONV_A):
        groups.setdefault((first_row + k) % SUBLANES, []).append(k)
    acc = jnp.zeros((rows, LANES), F32)
    for shift, ks in groups.items():
        lo = first_row + ks[0] - shift + r0
        src = zp_ref.at[0] if shift == 0 else zsh_ref.at[shift - 1, 0]
        window = src[pl.ds(lo, rows + ks[-1] - ks[0]), lanes]
        for k in ks:
            acc = acc + w_ref[pl.ds(k, 1), lanes] * window[k - ks[0]:k - ks[0] + rows]
    out_ref[pl.ds(r0, rows), lanes] = acc + bias_ref[:, lanes]


def _spatial_mix(vn, wsp_ref, bsp_ref, n_seq, n_rows):
    t = min(n_rows, MLP_CHUNK)
    r_total = n_seq * n_rows
    tril = lax.broadcasted_iota(jnp.int32, (t, t), 0) >= lax.broadcasted_iota(jnp.int32, (t, t), 1)
    heads = []
    if t == MLP_CHUNK:
        n_chunks = r_total // t
        bias = bsp_ref[...]
        for h in range(H_C):
            cols = slice(h * C_HEAD, (h + 1) * C_HEAD)
            wm = jnp.where(tril, wsp_ref[h], 0.0).astype(BF16)
            rhs = jnp.concatenate([vn[j * t:(j + 1) * t, cols] for j in range(n_chunks)], axis=1)
            mh = _dot(wm, rhs) + bias[:, h:h + 1]
            heads.append(jnp.concatenate([mh[:, j * C_HEAD:(j + 1) * C_HEAD] for j in range(n_chunks)], axis=0))
    else:
        sel = (lax.broadcasted_iota(jnp.int32, (r_total, t), 0) % t
               == lax.broadcasted_iota(jnp.int32, (r_total, t), 1)).astype(F32)
        sel_t = (lax.broadcasted_iota(jnp.int32, (t, r_total), 1) % t
                 == lax.broadcasted_iota(jnp.int32, (t, r_total), 0)).astype(F32)
        same_seq = (lax.broadcasted_iota(jnp.int32, (r_total, r_total), 0) // t
                    == lax.broadcasted_iota(jnp.int32, (r_total, r_total), 1) // t)
        bias = jnp.concatenate([bsp_ref[0:t, :]] * n_seq, axis=0)
        for h in range(H_C):
            cols = slice(h * C_HEAD, (h + 1) * C_HEAD)
            wm = jnp.where(tril, wsp_ref[h, 0:t, 0:t], 0.0)
            big = jnp.where(same_seq, _dot(_dot(sel, wm), sel_t), 0.0).astype(BF16)
            heads.append(_dot(big, vn[:, cols]) + bias[:, h:h + 1])
    return jnp.concatenate(heads, axis=1)


def _mixer_tile(n_seq, n_rows, fresh, x_ref, hista_ref, histb_ref, params, o_ref, newa_ref, newb_ref,
                vn_ref, zpa_ref, zpb_ref, zsh_ref, h_ref, proj_ref, ca_ref=None):
    (g_ref, win_ref, gbias_ref, caw_ref, cab_ref, lnag_ref, lnab_ref, waout_ref, cbw_ref, wbout_ref,
     lncg_ref, lncb_ref, wsp_ref, bsp_ref, wcout_ref, wo_ref) = params
    r_total = n_seq * n_rows

    @pl.when(fresh)
    def _():
        zpa_ref[:, 0:HIST_A_PAD, :] = hista_ref[...]
        zpb_ref[:, 0:HIST_B_PAD, :] = histb_ref[...]

    h_ref[...] = _rmsnorm(x_ref[...], g_ref[...]).astype(BF16)

    proj_ref[:, COL_A:COL_B] = _dot(h_ref[...], win_ref[:, COL_A:COL_B])
    def glu_into_conv_window():
        za = proj_ref[:, 0:W_BR] * jax.nn.sigmoid(proj_ref[:, W_BR:COL_B])
        zpa_ref[:, HIST_A_PAD:HIST_A_PAD + n_rows, :] = za.reshape(n_seq, n_rows, W_BR)

    if ca_ref is None:
        glu_into_conv_window()
        proj_ref[:, COL_B:COL_G] = _dot(h_ref[...], win_ref[:, COL_B:COL_G])
        ca = _causal_dwconv(zpa_ref, zsh_ref, caw_ref, CONV_A, HIST_A_PAD - (CONV_A - 1), n_seq, n_rows)
        ca = ca.reshape(r_total, W_BR) + cab_ref[...]
        first_gate_col = COL_G
    else:
        strips = [(r0, c0) for r0 in range(0, n_rows, CONV_STRIP_ROWS) for c0 in range(0, W_BR, LANES)]
        cols = list(range(COL_B, COL_B + CHUNKS_WITH_CONV * PROJ_CHUNK, PROJ_CHUNK))
        first_gate_col = cols[-1] + PROJ_CHUNK
        assert n_seq == 1 and COL_G <= first_gate_col <= COL_END

        @pl.when(pl.program_id(0) < GRID_BOUND)
        def _():
            glu_into_conv_window()
            for r in range(1, SUBLANES):
                zsh_ref[r - 1] = zpa_ref[:, pl.ds(r, zsh_ref.shape[2]), :]
            for i in range(max(len(cols), len(strips))):
                if i < len(cols):
                    chunk = slice(cols[i], cols[i] + PROJ_CHUNK)
                    proj_ref[:, chunk] = _dot(h_ref[...], win_ref[:, chunk])
                if i < len(strips):
                    _conv_a_strip(zpa_ref, zsh_ref, caw_ref, cab_ref, ca_ref, *strips[i], CONV_STRIP_ROWS)

        ca = ca_ref[...]
    ya_in = jax.nn.silu(_layernorm(ca, lnag_ref[...], lnab_ref[...])).astype(BF16)
    tail_a = zpa_ref[:, n_rows:n_rows + HIST_A_PAD, :]
    zpa_ref[:, 0:HIST_A_PAD, :] = tail_a
    newa_ref[...] = tail_a

    if first_gate_col < COL_END:
        proj_ref[:, first_gate_col:COL_END] = _dot(h_ref[...], win_ref[:, first_gate_col:COL_END])

    zb = proj_ref[:, COL_B + W_BR:COL_B + 2 * W_BR] * proj_ref[:, COL_B + 2 * W_BR:COL_C]
    zpb_ref[:, HIST_B_PAD:HIST_B_PAD + n_rows, :] = zb.reshape(n_seq, n_rows, W_BR)
    cb = _causal_dwconv(zpb_ref, None, cbw_ref, CONV_B, HIST_B_PAD - (CONV_B - 1), n_seq, n_rows)
    yb_in = (proj_ref[:, COL_B:COL_B + W_BR] * cb.reshape(r_total, W_BR)).astype(BF16)
    tail_b = zpb_ref[:, n_rows:n_rows + HIST_B_PAD, :]
    zpb_ref[:, 0:HIST_B_PAD, :] = tail_b
    newb_ref[...] = tail_b

    vn = _layernorm(jax.nn.gelu(proj_ref[:, COL_C + W_BR:COL_G]), lncg_ref[...], lncb_ref[...])
    if vn_ref is not None:
        vn_ref[...] = vn
    gu = jax.nn.gelu(proj_ref[:, COL_C:COL_C + W_BR])

    ya = _dot(ya_in, waout_ref[...])
    yb = _dot(yb_in, wbout_ref[...])
    mixed = _spatial_mix(vn.astype(BF16), wsp_ref, bsp_ref, n_seq, n_rows)
    yc = _dot((gu * mixed).astype(BF16), wcout_ref[...])

    gates = jax.nn.sigmoid(proj_ref[:, COL_G:COL_END] + gbias_ref[...])
    merged = (gates[:, :D_MODEL] * ya + gates[:, D_MODEL:2 * D_MODEL] * yb
              + gates[:, 2 * D_MODEL:] * yc)
    o_ref[...] = x_ref[...] + _dot(merged.astype(BF16), wo_ref[...])


N_MIXER_PARAMS = 16


def _mixer_kernel(n_prompt_steps, tiles_per_seq, prompt_rows, dec_batch, dec_seq,
                  xp_ref, xs_ref, hap_ref, has_ref, hbp_ref, hbs_ref, *refs):
    params = refs[:N_MIXER_PARAMS]
    (op_ref, os_ref, nap_ref, nas_ref, nbp_ref, nbs_ref, vn_ref,
     zpa_p, zpb_p, zsh_p, zpa_s, zpb_s, zsh_s, h_ref, proj_ref, ca_ref) = refs[N_MIXER_PARAMS:]
    step = pl.program_id(0)
    _two_groups(
        n_prompt_steps,
        lambda: _mixer_tile(1, prompt_rows, step % tiles_per_seq == 0, xp_ref, hap_ref, hbp_ref, params,
                            op_ref, nap_ref, nbp_ref, None, zpa_p, zpb_p, zsh_p, h_ref, proj_ref, ca_ref),
        lambda: _mixer_tile(dec_batch, dec_seq, step == n_prompt_steps, xs_ref, has_ref, hbs_ref, params,
                            os_ref, nas_ref, nbs_ref, vn_ref, zpa_s, zpb_s, zsh_s, h_ref, proj_ref))


class _Cast(NamedTuple):
    src: jax.Array
    layer: int


def _cast_chunks(rows, n_steps):
    units = rows // BF16_ROWS
    n_chunks = max(d for d in range(1, min(units, n_steps) + 1) if units % d == 0)
    return rows // n_chunks, n_chunks


def _with_casts(body, n_in, n_out, n_casts, *refs):
    ins, cast_ins = refs[:n_in], refs[n_in:n_in + n_casts]
    outs = refs[n_in + n_casts:n_in + n_casts + n_out]
    cast_outs = refs[n_in + n_casts + n_out:n_in + 2 * n_casts + n_out]
    body(*ins, *outs, *refs[n_in + 2 * n_casts + n_out:])
    for src_ref, dst_ref in zip(cast_ins, cast_outs):
        dst_ref[...] = src_ref[...].astype(BF16)


def _fused_call(body, name, grid, operands, in_specs, out_shapes, out_specs, scratch, casts):
    n_steps = 1
    for g in grid:
        n_steps *= g
    linear = (lambda t: t) if len(grid) == 1 else (lambda b, t: b * grid[1] + t)
    c_in, c_out, c_shapes = [], [], []
    for c in casts:
        _, rows, cols = c.src.shape
        chunk, n_chunks = _cast_chunks(rows, n_steps)
        at = lambda *g, n=n_chunks: jnp.minimum(linear(*g), n - 1)
        c_in.append(pl.BlockSpec((None, chunk, cols), lambda *g, at=at, layer=c.layer: (layer, at(*g), 0)))
        c_out.append(pl.BlockSpec((chunk, cols), lambda *g, at=at: (at(*g), 0)))
        c_shapes.append(jax.ShapeDtypeStruct((rows, cols), BF16))
    outs = pl.pallas_call(
        functools.partial(_with_casts, body, len(operands), len(out_shapes), len(casts)),
        out_shape=tuple(out_shapes) + tuple(c_shapes),
        grid=grid,
        in_specs=list(in_specs) + c_in,
        out_specs=tuple(out_specs) + tuple(c_out),
        scratch_shapes=scratch,
        compiler_params=pltpu.CompilerParams(dimension_semantics=("arbitrary",) * len(grid),
                                             vmem_limit_bytes=VMEM_LIMIT_BYTES),
        name=name,
    )(*operands, *[c.src for c in casts])
    return outs[:len(out_shapes)], outs[len(out_shapes):]


def _layer_spec(layer, tail_shape):
    zeros = (0,) * len(tail_shape)
    return pl.BlockSpec((None,) + tuple(tail_shape), lambda *g: (layer,) + zeros,
                        pipeline_mode=pl.Buffered(1))


def _resident_spec(shape):
    return pl.BlockSpec(tuple(shape), lambda *g: (0, 0), pipeline_mode=pl.Buffered(1))


def _tile_spec(rows, width, n_prompt_steps):
    return pl.BlockSpec((rows, width), lambda t: (jnp.minimum(t, n_prompt_steps - 1), 0))


def _whole_spec(shape):
    return pl.BlockSpec(tuple(shape), lambda t: (0,) * len(shape))


def _ffn_call(xp, xs, layer, w, wb, tm, casts):
    n_p = xp.shape[0] // tm
    outs, cast_out = _fused_call(
        functools.partial(_ffn_kernel, n_p), "ffn1", (n_p + 1,),
        [xp, xs, w['ffn1_norm'], wb['ffn1_w_gate'], wb['ffn1_w_up'], wb['ffn1_w_down']],
        [_tile_spec(tm, D_MODEL, n_p), _whole_spec(xs.shape), _layer_spec(layer, (1, D_MODEL)),
         _resident_spec((D_MODEL, D_FF)), _resident_spec((D_MODEL, D_FF)), _resident_spec((D_FF, D_MODEL))],
        [jax.ShapeDtypeStruct(xp.shape, F32), jax.ShapeDtypeStruct(xs.shape, F32)],
        [_tile_spec(tm, D_MODEL, n_p), _whole_spec(xs.shape)], [], casts)
    return outs, cast_out


def _ffn_ple_call(xp, xs, pp, ps, layer, w, wb, tm, final, casts):
    n_p = xp.shape[0] // tm
    outs, cast_out = _fused_call(
        functools.partial(_ffn_ple_kernel, final, n_p), "ffn2_ple", (n_p + 1,),
        [xp, xs, pp, ps, w['ffn2_norm'], wb['ffn2_w_gate'], wb['ffn2_w_up'], wb['ffn2_w_down'],
         w['ple_norm'], wb['w_ple_gate'], wb['w_ple_proj'], w['final_norm']],
        [_tile_spec(tm, D_MODEL, n_p), _whole_spec(xs.shape),
         pl.BlockSpec((None, tm, D_PLE), lambda t: (layer, jnp.minimum(t, n_p - 1), 0)),
         pl.BlockSpec((None,) + ps.shape[1:], lambda t: (layer, 0, 0)),
         _layer_spec(layer, (1, D_MODEL)), _resident_spec((D_MODEL, D_FF)),
         _resident_spec((D_MODEL, D_FF)), _resident_spec((D_FF, D_MODEL)),
         _layer_spec(layer, (1, D_MODEL)), _resident_spec((D_MODEL, D_MODEL)),
         _resident_spec((D_PLE, D_MODEL)), _resident_spec((1, D_MODEL))],
        [jax.ShapeDtypeStruct(xp.shape, F32), jax.ShapeDtypeStruct(xs.shape, F32)],
        [_tile_spec(tm, D_MODEL, n_p), _whole_spec(xs.shape)], [], casts)
    return outs, cast_out


def _mixer_call(xp, xs, hist_ap, hist_as, hist_bp, hist_bs, layer, w, wb, seq, dec_seq, tile_rows, casts):
    n_p = xp.shape[0] // tile_rows
    tiles_per_seq = seq // tile_rows
    dec_batch = xs.shape[0] // dec_seq
    assert xs.shape[0] == tile_rows
    ls = functools.partial(_layer_spec, layer)
    seq_of_step = lambda t: jnp.minimum(t, n_p - 1) // tiles_per_seq
    state_p = lambda pad: pl.BlockSpec((1, pad, W_BR), lambda t: (seq_of_step(t), 0, 0))
    state_s = lambda pad: _whole_spec((dec_batch, pad, W_BR))
    in_specs = [
        _tile_spec(tile_rows, D_MODEL, n_p), _whole_spec(xs.shape),
        state_p(HIST_A_PAD), state_s(HIST_A_PAD), state_p(HIST_B_PAD), state_s(HIST_B_PAD),
        ls((1, D_MODEL)), _resident_spec((D_MODEL, COL_END)), ls((1, COL_END - COL_G)),
        ls((CONV_A, W_BR)), ls((1, W_BR)), ls((1, W_BR)), ls((1, W_BR)), _resident_spec((W_BR, D_MODEL)),
        ls((CONV_B, W_BR)), _resident_spec((W_BR, D_MODEL)),
        ls((1, W_BR)), ls((1, W_BR)), ls((H_C, MLP_CHUNK, MLP_CHUNK)), ls((MLP_CHUNK, H_C)),
        _resident_spec((W_BR, D_MODEL)), _resident_spec((D_MODEL, D_MODEL)),
    ]
    f32 = lambda *shape: jax.ShapeDtypeStruct(shape, F32)
    out_shapes = [f32(*xp.shape), f32(*xs.shape),
                  f32(hist_ap.shape[0], HIST_A_PAD, W_BR), f32(dec_batch, HIST_A_PAD, W_BR),
                  f32(hist_bp.shape[0], HIST_B_PAD, W_BR), f32(dec_batch, HIST_B_PAD, W_BR),
                  f32(xs.shape[0], W_BR)]
    out_specs = [_tile_spec(tile_rows, D_MODEL, n_p), _whole_spec(xs.shape),
                 state_p(HIST_A_PAD), state_s(HIST_A_PAD), state_p(HIST_B_PAD), state_s(HIST_B_PAD),
                 _whole_spec((xs.shape[0], W_BR))]

    def conv_scratch(n_seq, n_rows):
        return [pltpu.VMEM((n_seq, HIST_A_PAD + n_rows, W_BR), F32),
                pltpu.VMEM((n_seq, HIST_B_PAD + n_rows, W_BR), F32),
                pltpu.VMEM((SUBLANES - 1, n_seq, HIST_A_PAD - SUBLANES + n_rows, W_BR), F32)]

    scratch = (conv_scratch(1, tile_rows) + conv_scratch(dec_batch, dec_seq)
               + [pltpu.VMEM((tile_rows, D_MODEL), BF16), pltpu.VMEM((tile_rows, COL_END), F32),
                  pltpu.VMEM((tile_rows, W_BR), F32)])
    params = [w['mix_norm'], wb['w_in'], w['gate_bias'],
              w['conv_a_w'], w['conv_a_b'], w['ln_a_g'], w['ln_a_b'], wb['w_a_out'],
              w['conv_b_w'], wb['w_b_out'], w['ln_c_g'], w['ln_c_b'], w['w_spatial'], w['b_spatial'],
              wb['w_c_out'], wb['w_o']]
    assert len(params) == N_MIXER_PARAMS
    return _fused_call(
        functools.partial(_mixer_kernel, n_p, tiles_per_seq, tile_rows, dec_batch, dec_seq), "mixer",
        (n_p + 1,), [xp, xs, hist_ap, hist_as, hist_bp, hist_bs] + params,
        in_specs, out_shapes, out_specs, scratch, casts)


FFN1_WEIGHTS = ('ffn1_w_gate', 'ffn1_w_up', 'ffn1_w_down')
MIXER_WEIGHTS = ('w_in', 'w_a_out', 'w_b_out', 'w_c_out', 'w_o')
FFN2_WEIGHTS = ('ffn2_w_gate', 'ffn2_w_up', 'ffn2_w_down', 'w_ple_gate', 'w_ple_proj')
CAST_ONLY_STEPS = 16


MIXER_TILE_ROWS = 256
FFN_TILE_ROWS = 512


def _trunk(xp, xs, pp, ps, hist_ap, hist_as, hist_bp, hist_bs, w, raw, depth, seq, dec_seq):
    casts = lambda names, layer: [_Cast(raw[n], layer) for n in names] if layer < depth else []
    wb = [dict() for _ in range(depth)]
    _, done = _fused_call(lambda: None, "cast", (CAST_ONLY_STEPS,), [], [], [], [], [], casts(FFN1_WEIGHTS, 0))
    wb[0].update(zip(FFN1_WEIGHTS, done))
    states, chunk_v = [], []
    for i in range(depth):
        (xp, xs), done = _ffn_call(xp, xs, i, w, wb[i], FFN_TILE_ROWS, casts(MIXER_WEIGHTS, i))
        wb[i].update(zip(MIXER_WEIGHTS, done))
        (xp, xs, nap, nas, nbp, nbs, vn), done = _mixer_call(
            xp, xs, hist_ap[i], hist_as[i], hist_bp[i], hist_bs[i], i, w, wb[i], seq, dec_seq,
            MIXER_TILE_ROWS, casts(FFN2_WEIGHTS, i))
        wb[i].update(zip(FFN2_WEIGHTS, done))
        (xp, xs), done = _ffn_ple_call(xp, xs, pp, ps, i, w, wb[i], FFN_TILE_ROWS, i == depth - 1,
                                       casts(FFN1_WEIGHTS, i + 1))
        if done:
            wb[i + 1].update(zip(FFN1_WEIGHTS, done))
        states.append([nap[:, HIST_A_PAD - (CONV_A - 1):], nas[:, HIST_A_PAD - (CONV_A - 1):],
                       nbp[:, HIST_B_PAD - (CONV_B - 1):], nbs[:, HIST_B_PAD - (CONV_B - 1):]])
        chunk_v.append(vn)
    return (xp, xs, *[jnp.stack(s) for s in zip(*states)], jnp.stack(chunk_v))


def _front_pad(hist, pad_to):
    return jnp.pad(hist, ((0, 0), (0, 0), (pad_to - hist.shape[2], 0), (0, 0)))


def kernel(x_prompt, x_sample, p_prompt, p_sample, cache_conv_a, cache_conv_b, ffn1_norm, ffn1_w_gate, ffn1_w_up, ffn1_w_down, mix_norm, w_in, gate_bias, conv_a_w, conv_a_b, ln_a_g, ln_a_b, w_a_out, conv_b_w, w_b_out, ln_c_g, ln_c_b, w_spatial, b_spatial, w_c_out, w_o, ffn2_norm, ffn2_w_gate, ffn2_w_up, ffn2_w_down, ple_norm, w_ple_gate, w_ple_proj, final_norm):
    depth = w_in.shape[0]
    batch, seq, _ = x_prompt.shape
    dec_batch, dec_seq, _ = x_sample.shape
    row = lambda a: a.reshape(a.shape[0], 1, a.shape[1])
    w = dict(
        ffn1_norm=row(ffn1_norm), mix_norm=row(mix_norm), gate_bias=row(gate_bias),
        conv_a_w=conv_a_w, conv_a_b=row(conv_a_b), ln_a_g=row(ln_a_g), ln_a_b=row(ln_a_b),
        conv_b_w=conv_b_w, ln_c_g=row(ln_c_g), ln_c_b=row(ln_c_b), w_spatial=w_spatial,
        b_spatial=jnp.swapaxes(b_spatial, 1, 2), ffn2_norm=row(ffn2_norm), ple_norm=row(ple_norm),
        final_norm=final_norm.reshape(1, D_MODEL))
    raw = dict(ffn1_w_gate=ffn1_w_gate, ffn1_w_up=ffn1_w_up, ffn1_w_down=ffn1_w_down,
               w_in=w_in, w_a_out=w_a_out, w_b_out=w_b_out, w_c_out=w_c_out, w_o=w_o,
               ffn2_w_gate=ffn2_w_gate, ffn2_w_up=ffn2_w_up, ffn2_w_down=ffn2_w_down,
               w_ple_gate=w_ple_gate, w_ple_proj=w_ple_proj)

    (y_prompt, y_sample, conv_a_prompt, conv_a_sample, conv_b_prompt, conv_b_sample, chunk_v) = _trunk(
        x_prompt.reshape(batch * seq, D_MODEL), x_sample.reshape(dec_batch * dec_seq, D_MODEL),
        p_prompt.reshape(depth, batch * seq, D_PLE), p_sample.reshape(depth, dec_batch * dec_seq, D_PLE),
        jnp.zeros((depth, batch, HIST_A_PAD, W_BR), F32), _front_pad(cache_conv_a, HIST_A_PAD),
        jnp.zeros((depth, batch, HIST_B_PAD, W_BR), F32), _front_pad(cache_conv_b, HIST_B_PAD),
        w, raw, depth, seq, dec_seq)

    return (y_prompt.reshape(x_prompt.shape), y_sample.reshape(x_sample.shape),
            conv_a_prompt, conv_a_sample, conv_b_prompt, conv_b_sample,
            chunk_v.reshape(depth, dec_batch, dec_seq, W_BR))
```

```python
import functools
from typing import NamedTuple

import jax
import jax.numpy as jnp
from jax import lax
from jax.experimental import pallas as pl
from jax.experimental.pallas import tpu as pltpu

D_MODEL = 1024
D_FF = 2816
D_PLE = 256
W_BR = 512
CONV_A = 31
CONV_B = 3
H_C = 4
C_HEAD = W_BR // H_C
MLP_CHUNK = 128
EPS = 1e-6
COL_A, COL_B, COL_C, COL_G, COL_END = 0, 1024, 2560, 3584, 6656

HIST_A_PAD = 32
HIST_B_PAD = 8
SUBLANES = 8
BF16_ROWS = 16
LANES = 128
CONV_STRIP_ROWS = 128
PROJ_CHUNK = 512
CHUNKS_WITH_CONV = 9
GRID_BOUND = 1 << 30
CONV_ACC_VREGS = 16
VMEM_LIMIT_BYTES = 56 * 1024 * 1024

BF16 = jnp.bfloat16
F32 = jnp.float32


def _dot(a, b):
    return jnp.dot(a, b, preferred_element_type=F32)


def _rmsnorm(x, g):
    return x * lax.rsqrt(jnp.mean(x * x, axis=-1, keepdims=True) + EPS) * g


def _layernorm(x, g, b):
    mu = jnp.mean(x, axis=-1, keepdims=True)
    xc = x - mu
    return xc * lax.rsqrt(jnp.mean(xc * xc, axis=-1, keepdims=True) + EPS) * g + b


def _swiglu_half_step(x, g, wg_ref, wu_ref, wd_ref):
    h = _rmsnorm(x, g).astype(BF16)
    act = jax.nn.silu(_dot(h, wg_ref[...])) * _dot(h, wu_ref[...])
    return x + 0.5 * _dot(act.astype(BF16), wd_ref[...])


def _two_groups(n_prompt_steps, prompt_body, sample_body):
    step = pl.program_id(0)
    pl.when(step < n_prompt_steps)(prompt_body)
    pl.when(step == n_prompt_steps)(sample_body)


def _ffn_kernel(n_prompt_steps, xp_ref, xs_ref, g_ref, wg_ref, wu_ref, wd_ref, op_ref, os_ref):
    def tile(x_ref, o_ref):
        o_ref[...] = _swiglu_half_step(x_ref[...], g_ref[...], wg_ref, wu_ref, wd_ref)

    _two_groups(n_prompt_steps, functools.partial(tile, xp_ref, op_ref), functools.partial(tile, xs_ref, os_ref))


def _ffn_ple_kernel(final, n_prompt_steps, xp_ref, xs_ref, pp_ref, ps_ref, g_ref, wg_ref, wu_ref, wd_ref,
                    pg_ref, wpg_ref, wpp_ref, fg_ref, op_ref, os_ref):
    def tile(x_ref, p_ref, o_ref):
        x = _swiglu_half_step(x_ref[...], g_ref[...], wg_ref, wu_ref, wd_ref)
        gate = jax.nn.sigmoid(_dot(_rmsnorm(x, pg_ref[...]).astype(BF16), wpg_ref[...]))
        x = x + gate * _dot(p_ref[...].astype(BF16), wpp_ref[...])
        if final:
            x = _rmsnorm(x, fg_ref[...])
        o_ref[...] = x

    _two_groups(n_prompt_steps, functools.partial(tile, xp_ref, pp_ref, op_ref),
                functools.partial(tile, xs_ref, ps_ref, os_ref))


def _causal_dwconv(zp_ref, zsh_ref, w_ref, n_taps, first_row, n_seq, n_rows):
    if zsh_ref is not None:
        span = zsh_ref.shape[2]
        for r in range(1, SUBLANES):
            zsh_ref[r - 1] = zp_ref[:, pl.ds(r, span), :]
    rows = min(n_rows, 32)
    seqs = max(1, min(n_seq, CONV_ACC_VREGS * SUBLANES * LANES // (rows * W_BR)))
    w = w_ref[...]
    seq_parts = []
    for s0 in range(0, n_seq, seqs):
        row_parts = []
        for r0 in range(0, n_rows, rows):
            acc = jnp.zeros((seqs, rows, W_BR), F32)
            for k in range(n_taps):
                shift, base = (first_row + k) % SUBLANES, (first_row + k) // SUBLANES * SUBLANES
                if zsh_ref is None:
                    win = zp_ref[pl.ds(s0, seqs), pl.ds(first_row + r0 + k, rows), :]
                elif shift == 0:
                    win = zp_ref[pl.ds(s0, seqs), pl.ds(base + r0, rows), :]
                else:
                    win = zsh_ref[shift - 1, pl.ds(s0, seqs), pl.ds(base + r0, rows), :]
                acc = acc + w[k] * win
            row_parts.append(acc)
        seq_parts.append(row_parts[0] if len(row_parts) == 1 else jnp.concatenate(row_parts, axis=1))
    return seq_parts[0] if len(seq_parts) == 1 else jnp.concatenate(seq_parts, axis=0)


def _conv_a_strip(zp_ref, zsh_ref, w_ref, bias_ref, out_ref, r0, c, rows):
    first_row = HIST_A_PAD - (CONV_A - 1)
    lanes = pl.ds(c * LANES, LANES)
    groups = {}
    for k in range(CONV_A):
        groups.setdefault((first_row + k) % SUBLANES, []).append(k)
    acc = jnp.zeros((rows, LANES), F32)
    for shift, ks in groups.items():
        lo = first_row + ks[0] - shift + r0
        src = zp_ref.at[c] if shift == 0 else zsh_ref.at[shift - 1, c]
        window = src[pl.ds(lo, rows + ks[-1] - ks[0]), :]
        for k in ks:
            acc = acc + w_ref[pl.ds(k, 1), lanes] * window[k - ks[0]:k - ks[0] + rows]
    out_ref[c, pl.ds(r0, rows), :] = acc + bias_ref[:, lanes]


def _spatial_mix(vn, wsp_ref, bsp_ref, n_seq, n_rows):
    t = min(n_rows, MLP_CHUNK)
    r_total = n_seq * n_rows
    tril = lax.broadcasted_iota(jnp.int32, (t, t), 0) >= lax.broadcasted_iota(jnp.int32, (t, t), 1)
    heads = []
    if t == MLP_CHUNK:
        n_chunks = r_total // t
        bias = bsp_ref[...]
        for h in range(H_C):
            cols = slice(h * C_HEAD, (h + 1) * C_HEAD)
            wm = jnp.where(tril, wsp_ref[h], 0.0).astype(BF16)
            rhs = jnp.concatenate([vn[j * t:(j + 1) * t, cols] for j in range(n_chunks)], axis=1)
            mh = _dot(wm, rhs) + bias[:, h:h + 1]
            heads.append(jnp.concatenate([mh[:, j * C_HEAD:(j + 1) * C_HEAD] for j in range(n_chunks)], axis=0))
    else:
        sel = (lax.broadcasted_iota(jnp.int32, (r_total, t), 0) % t
               == lax.broadcasted_iota(jnp.int32, (r_total, t), 1)).astype(F32)
        sel_t = (lax.broadcasted_iota(jnp.int32, (t, r_total), 1) % t
                 == lax.broadcasted_iota(jnp.int32, (t, r_total), 0)).astype(F32)
        same_seq = (lax.broadcasted_iota(jnp.int32, (r_total, r_total), 0) // t
                    == lax.broadcasted_iota(jnp.int32, (r_total, r_total), 1) // t)
        bias = jnp.concatenate([bsp_ref[0:t, :]] * n_seq, axis=0)
        for h in range(H_C):
            cols = slice(h * C_HEAD, (h + 1) * C_HEAD)
            wm = jnp.where(tril, wsp_ref[h, 0:t, 0:t], 0.0)
            big = jnp.where(same_seq, _dot(_dot(sel, wm), sel_t), 0.0).astype(BF16)
            heads.append(_dot(big, vn[:, cols]) + bias[:, h:h + 1])
    return jnp.concatenate(heads, axis=1)


def _mixer_tile(n_seq, n_rows, fresh, x_ref, hista_ref, histb_ref, params, o_ref, newa_ref, newb_ref,
                vn_ref, zpa_ref, zpb_ref, zsh_ref, h_ref, proj_ref, ca_ref=None):
    (g_ref, win_ref, gbias_ref, caw_ref, cab_ref, lnag_ref, lnab_ref, waout_ref, cbw_ref, wbout_ref,
     lncg_ref, lncb_ref, wsp_ref, bsp_ref, wcout_ref, wo_ref) = params
    r_total = n_seq * n_rows

    lane_tiles = [slice(c0, c0 + LANES) for c0 in range(0, W_BR, LANES)]

    def put_conv_a_rows(first, value):
        if ca_ref is None:
            zpa_ref[:, first:first + value.shape[-2], :] = value
        else:
            for c, lanes in enumerate(lane_tiles):
                zpa_ref[c, first:first + value.shape[-2], :] = value.reshape(-1, W_BR)[:, lanes]

    @pl.when(fresh)
    def _():
        put_conv_a_rows(0, hista_ref[...])
        zpb_ref[:, 0:HIST_B_PAD, :] = histb_ref[...]

    h_ref[...] = _rmsnorm(x_ref[...], g_ref[...]).astype(BF16)

    proj_ref[:, COL_A:COL_B] = _dot(h_ref[...], win_ref[:, COL_A:COL_B])
    def glu_into_conv_window():
        za = proj_ref[:, 0:W_BR] * jax.nn.sigmoid(proj_ref[:, W_BR:COL_B])
        put_conv_a_rows(HIST_A_PAD, za.reshape(n_seq, n_rows, W_BR))

    if ca_ref is None:
        glu_into_conv_window()
        proj_ref[:, COL_B:COL_G] = _dot(h_ref[...], win_ref[:, COL_B:COL_G])
        ca = _causal_dwconv(zpa_ref, zsh_ref, caw_ref, CONV_A, HIST_A_PAD - (CONV_A - 1), n_seq, n_rows)
        ca = ca.reshape(r_total, W_BR) + cab_ref[...]
        first_gate_col = COL_G
    else:
        strips = [(r0, c) for r0 in range(0, n_rows, CONV_STRIP_ROWS) for c in range(len(lane_tiles))]
        cols = list(range(COL_B, COL_B + CHUNKS_WITH_CONV * PROJ_CHUNK, PROJ_CHUNK))
        first_gate_col = cols[-1] + PROJ_CHUNK
        assert n_seq == 1 and COL_G <= first_gate_col <= COL_END

        @pl.when(pl.program_id(0) < GRID_BOUND)
        def _():
            glu_into_conv_window()
            for r in range(1, SUBLANES):
                for c in range(len(lane_tiles)):
                    zsh_ref[r - 1, c] = zpa_ref[c, pl.ds(r, zsh_ref.shape[2]), :]
            for i in range(max(len(cols), len(strips))):
                if i < len(cols):
                    chunk = slice(cols[i], cols[i] + PROJ_CHUNK)
                    proj_ref[:, chunk] = _dot(h_ref[...], win_ref[:, chunk])
                if i < len(strips):
                    _conv_a_strip(zpa_ref, zsh_ref, caw_ref, cab_ref, ca_ref, *strips[i], CONV_STRIP_ROWS)

        ca = jnp.concatenate([ca_ref[c] for c in range(len(lane_tiles))], axis=1)
    ya_in = jax.nn.silu(_layernorm(ca, lnag_ref[...], lnab_ref[...])).astype(BF16)
    if ca_ref is None:
        tail_a = zpa_ref[:, n_rows:n_rows + HIST_A_PAD, :]
    else:
        tail_a = jnp.concatenate([zpa_ref[c, n_rows:n_rows + HIST_A_PAD, :]
                                  for c in range(len(lane_tiles))], axis=1)[None]
    put_conv_a_rows(0, tail_a)
    newa_ref[...] = tail_a

    if first_gate_col < COL_END:
        proj_ref[:, first_gate_col:COL_END] = _dot(h_ref[...], win_ref[:, first_gate_col:COL_END])

    zb = proj_ref[:, COL_B + W_BR:COL_B + 2 * W_BR] * proj_ref[:, COL_B + 2 * W_BR:COL_C]
    zpb_ref[:, HIST_B_PAD:HIST_B_PAD + n_rows, :] = zb.reshape(n_seq, n_rows, W_BR)
    cb = _causal_dwconv(zpb_ref, None, cbw_ref, CONV_B, HIST_B_PAD - (CONV_B - 1), n_seq, n_rows)
    yb_in = (proj_ref[:, COL_B:COL_B + W_BR] * cb.reshape(r_total, W_BR)).astype(BF16)
    tail_b = zpb_ref[:, n_rows:n_rows + HIST_B_PAD, :]
    zpb_ref[:, 0:HIST_B_PAD, :] = tail_b
    newb_ref[...] = tail_b

    vn = _layernorm(jax.nn.gelu(proj_ref[:, COL_C + W_BR:COL_G]), lncg_ref[...], lncb_ref[...])
    if vn_ref is not None:
        vn_ref[...] = vn
    gu = jax.nn.gelu(proj_ref[:, COL_C:COL_C + W_BR])

    ya = _dot(ya_in, waout_ref[...])
    yb = _dot(yb_in, wbout_ref[...])
    mixed = _spatial_mix(vn.astype(BF16), wsp_ref, bsp_ref, n_seq, n_rows)
    yc = _dot((gu * mixed).astype(BF16), wcout_ref[...])

    gates = jax.nn.sigmoid(proj_ref[:, COL_G:COL_END] + gbias_ref[...])
    merged = (gates[:, :D_MODEL] * ya + gates[:, D_MODEL:2 * D_MODEL] * yb
              + gates[:, 2 * D_MODEL:] * yc)
    o_ref[...] = x_ref[...] + _dot(merged.astype(BF16), wo_ref[...])


N_MIXER_PARAMS = 16


def _mixer_kernel(n_prompt_steps, tiles_per_seq, prompt_rows, dec_batch, dec_seq,
                  xp_ref, xs_ref, hap_ref, has_ref, hbp_ref, hbs_ref, *refs):
    params = refs[:N_MIXER_PARAMS]
    (op_ref, os_ref, nap_ref, nas_ref, nbp_ref, nbs_ref, vn_ref,
     zpa_p, zpb_p, zsh_p, zpa_s, zpb_s, zsh_s, h_ref, proj_ref, ca_ref) = refs[N_MIXER_PARAMS:]
    step = pl.program_id(0)
    _two_groups(
        n_prompt_steps,
        lambda: _mixer_tile(1, prompt_rows, step % tiles_per_seq == 0, xp_ref, hap_ref, hbp_ref, params,
                            op_ref, nap_ref, nbp_ref, None, zpa_p, zpb_p, zsh_p, h_ref, proj_ref, ca_ref),
        lambda: _mixer_tile(dec_batch, dec_seq, step == n_prompt_steps, xs_ref, has_ref, hbs_ref, params,
                            os_ref, nas_ref, nbs_ref, vn_ref, zpa_s, zpb_s, zsh_s, h_ref, proj_ref))


class _Cast(NamedTuple):
    src: jax.Array
    layer: int


def _cast_chunks(rows, n_steps):
    units = rows // BF16_ROWS
    n_chunks = max(d for d in range(1, min(units, n_steps) + 1) if units % d == 0)
    return rows // n_chunks, n_chunks


def _with_casts(body, n_in, n_out, n_casts, *refs):
    ins, cast_ins = refs[:n_in], refs[n_in:n_in + n_casts]
    outs = refs[n_in + n_casts:n_in + n_casts + n_out]
    cast_outs = refs[n_in + n_casts + n_out:n_in + 2 * n_casts + n_out]
    body(*ins, *outs, *refs[n_in + 2 * n_casts + n_out:])
    for src_ref, dst_ref in zip(cast_ins, cast_outs):
        dst_ref[...] = src_ref[...].astype(BF16)


def _fused_call(body, name, grid, operands, in_specs, out_shapes, out_specs, scratch, casts):
    n_steps = 1
    for g in grid:
        n_steps *= g
    linear = (lambda t: t) if len(grid) == 1 else (lambda b, t: b * grid[1] + t)
    c_in, c_out, c_shapes = [], [], []
    for c in casts:
        _, rows, cols = c.src.shape
        chunk, n_chunks = _cast_chunks(rows, n_steps)
        at = lambda *g, n=n_chunks: jnp.minimum(linear(*g), n - 1)
        c_in.append(pl.BlockSpec((None, chunk, cols), lambda *g, at=at, layer=c.layer: (layer, at(*g), 0)))
        c_out.append(pl.BlockSpec((chunk, cols), lambda *g, at=at: (at(*g), 0)))
        c_shapes.append(jax.ShapeDtypeStruct((rows, cols), BF16))
    outs = pl.pallas_call(
        functools.partial(_with_casts, body, len(operands), len(out_shapes), len(casts)),
        out_shape=tuple(out_shapes) + tuple(c_shapes),
        grid=grid,
        in_specs=list(in_specs) + c_in,
        out_specs=tuple(out_specs) + tuple(c_out),
        scratch_shapes=scratch,
        compiler_params=pltpu.CompilerParams(dimension_semantics=("arbitrary",) * len(grid),
                                             vmem_limit_bytes=VMEM_LIMIT_BYTES),
        name=name,
    )(*operands, *[c.src for c in casts])
    return outs[:len(out_shapes)], outs[len(out_shapes):]


def _layer_spec(layer, tail_shape):
    zeros = (0,) * len(tail_shape)
    return pl.BlockSpec((None,) + tuple(tail_shape), lambda *g: (layer,) + zeros,
                        pipeline_mode=pl.Buffered(1))


def _resident_spec(shape):
    return pl.BlockSpec(tuple(shape), lambda *g: (0, 0), pipeline_mode=pl.Buffered(1))


def _tile_spec(rows, width, n_prompt_steps):
    return pl.BlockSpec((rows, width), lambda t: (jnp.minimum(t, n_prompt_steps - 1), 0))


def _whole_spec(shape):
    return pl.BlockSpec(tuple(shape), lambda t: (0,) * len(shape))


def _ffn_call(xp, xs, layer, w, wb, tm, casts):
    n_p = xp.shape[0] // tm
    outs, cast_out = _fused_call(
        functools.partial(_ffn_kernel, n_p), "ffn1", (n_p + 1,),
        [xp, xs, w['ffn1_norm'], wb['ffn1_w_gate'], wb['ffn1_w_up'], wb['ffn1_w_down']],
        [_tile_spec(tm, D_MODEL, n_p), _whole_spec(xs.shape), _layer_spec(layer, (1, D_MODEL)),
         _resident_spec((D_MODEL, D_FF)), _resident_spec((D_MODEL, D_FF)), _resident_spec((D_FF, D_MODEL))],
        [jax.ShapeDtypeStruct(xp.shape, F32), jax.ShapeDtypeStruct(xs.shape, F32)],
        [_tile_spec(tm, D_MODEL, n_p), _whole_spec(xs.shape)], [], casts)
    return outs, cast_out


def _ffn_ple_call(xp, xs, pp, ps, layer, w, wb, tm, final, casts):
    n_p = xp.shape[0] // tm
    outs, cast_out = _fused_call(
        functools.partial(_ffn_ple_kernel, final, n_p), "ffn2_ple", (n_p + 1,),
        [xp, xs, pp, ps, w['ffn2_norm'], wb['ffn2_w_gate'], wb['ffn2_w_up'], wb['ffn2_w_down'],
         w['ple_norm'], wb['w_ple_gate'], wb['w_ple_proj'], w['final_norm']],
        [_tile_spec(tm, D_MODEL, n_p), _whole_spec(xs.shape),
         pl.BlockSpec((None, tm, D_PLE), lambda t: (layer, jnp.minimum(t, n_p - 1), 0)),
         pl.BlockSpec((None,) + ps.shape[1:], lambda t: (layer, 0, 0)),
         _layer_spec(layer, (1, D_MODEL)), _resident_spec((D_MODEL, D_FF)),
         _resident_spec((D_MODEL, D_FF)), _resident_spec((D_FF, D_MODEL)),
         _layer_spec(layer, (1, D_MODEL)), _resident_spec((D_MODEL, D_MODEL)),
         _resident_spec((D_PLE, D_MODEL)), _resident_spec((1, D_MODEL))],
        [jax.ShapeDtypeStruct(xp.shape, F32), jax.ShapeDtypeStruct(xs.shape, F32)],
        [_tile_spec(tm, D_MODEL, n_p), _whole_spec(xs.shape)], [], casts)
    return outs, cast_out


def _mixer_call(xp, xs, hist_ap, hist_as, hist_bp, hist_bs, layer, w, wb, seq, dec_seq, tile_rows, casts):
    n_p = xp.shape[0] // tile_rows
    tiles_per_seq = seq // tile_rows
    dec_batch = xs.shape[0] // dec_seq
    assert xs.shape[0] == tile_rows
    ls = functools.partial(_layer_spec, layer)
    seq_of_step = lambda t: jnp.minimum(t, n_p - 1) // tiles_per_seq
    state_p = lambda pad: pl.BlockSpec((1, pad, W_BR), lambda t: (seq_of_step(t), 0, 0))
    state_s = lambda pad: _whole_spec((dec_batch, pad, W_BR))
    in_specs = [
        _tile_spec(tile_rows, D_MODEL, n_p), _whole_spec(xs.shape),
        state_p(HIST_A_PAD), state_s(HIST_A_PAD), state_p(HIST_B_PAD), state_s(HIST_B_PAD),
        ls((1, D_MODEL)), _resident_spec((D_MODEL, COL_END)), ls((1, COL_END - COL_G)),
        ls((CONV_A, W_BR)), ls((1, W_BR)), ls((1, W_BR)), ls((1, W_BR)), _resident_spec((W_BR, D_MODEL)),
        ls((CONV_B, W_BR)), _resident_spec((W_BR, D_MODEL)),
        ls((1, W_BR)), ls((1, W_BR)), ls((H_C, MLP_CHUNK, MLP_CHUNK)), ls((MLP_CHUNK, H_C)),
        _resident_spec((W_BR, D_MODEL)), _resident_spec((D_MODEL, D_MODEL)),
    ]
    f32 = lambda *shape: jax.ShapeDtypeStruct(shape, F32)
    out_shapes = [f32(*xp.shape), f32(*xs.shape),
                  f32(hist_ap.shape[0], HIST_A_PAD, W_BR), f32(dec_batch, HIST_A_PAD, W_BR),
                  f32(hist_bp.shape[0], HIST_B_PAD, W_BR), f32(dec_batch, HIST_B_PAD, W_BR),
                  f32(xs.shape[0], W_BR)]
    out_specs = [_tile_spec(tile_rows, D_MODEL, n_p), _whole_spec(xs.shape),
                 state_p(HIST_A_PAD), state_s(HIST_A_PAD), state_p(HIST_B_PAD), state_s(HIST_B_PAD),
                 _whole_spec((xs.shape[0], W_BR))]

    def conv_scratch(lead, n_rows, width):
        return [pltpu.VMEM((lead, HIST_A_PAD + n_rows, width), F32),
                pltpu.VMEM((lead if width == W_BR else 1, HIST_B_PAD + n_rows, W_BR), F32),
                pltpu.VMEM((SUBLANES - 1, lead, HIST_A_PAD - SUBLANES + n_rows, width), F32)]

    scratch = (conv_scratch(W_BR // LANES, tile_rows, LANES) + conv_scratch(dec_batch, dec_seq, W_BR)
               + [pltpu.VMEM((tile_rows, D_MODEL), BF16), pltpu.VMEM((tile_rows, COL_END), F32),
                  pltpu.VMEM((W_BR // LANES, tile_rows, LANES), F32)])
    params = [w['mix_norm'], wb['w_in'], w['gate_bias'],
              w['conv_a_w'], w['conv_a_b'], w['ln_a_g'], w['ln_a_b'], wb['w_a_out'],
              w['conv_b_w'], wb['w_b_out'], w['ln_c_g'], w['ln_c_b'], w['w_spatial'], w['b_spatial'],
              wb['w_c_out'], wb['w_o']]
    assert len(params) == N_MIXER_PARAMS
    return _fused_call(
        functools.partial(_mixer_kernel, n_p, tiles_per_seq, tile_rows, dec_batch, dec_seq), "mixer",
        (n_p + 1,), [xp, xs, hist_ap, hist_as, hist_bp, hist_bs] + params,
        in_specs, out_shapes, out_specs, scratch, casts)


FFN1_WEIGHTS = ('ffn1_w_gate', 'ffn1_w_up', 'ffn1_w_down')
MIXER_WEIGHTS = ('w_in', 'w_a_out', 'w_b_out', 'w_c_out', 'w_o')
FFN2_WEIGHTS = ('ffn2_w_gate', 'ffn2_w_up', 'ffn2_w_down', 'w_ple_gate', 'w_ple_proj')
CAST_ONLY_STEPS = 16


MIXER_TILE_ROWS = 256
FFN_TILE_ROWS = 512


def _trunk(xp, xs, pp, ps, hist_ap, hist_as, hist_bp, hist_bs, w, raw, depth, seq, dec_seq):
    casts = lambda names, layer: [_Cast(raw[n], layer) for n in names] if layer < depth else []
    wb = [dict() for _ in range(depth)]
    _, done = _fused_call(lambda: None, "cast", (CAST_ONLY_STEPS,), [], [], [], [], [], casts(FFN1_WEIGHTS, 0))
    wb[0].update(zip(FFN1_WEIGHTS, done))
    states, chunk_v = [], []
    for i in range(depth):
        (xp, xs), done = _ffn_call(xp, xs, i, w, wb[i], FFN_TILE_ROWS, casts(MIXER_WEIGHTS, i))
        wb[i].update(zip(MIXER_WEIGHTS, done))
        (xp, xs, nap, nas, nbp, nbs, vn), done = _mixer_call(
            xp, xs, hist_ap[i], hist_as[i], hist_bp[i], hist_bs[i], i, w, wb[i], seq, dec_seq,
            MIXER_TILE_ROWS, casts(FFN2_WEIGHTS, i))
        wb[i].update(zip(FFN2_WEIGHTS, done))
        (xp, xs), done = _ffn_ple_call(xp, xs, pp, ps, i, w, wb[i], FFN_TILE_ROWS, i == depth - 1,
                                       casts(FFN1_WEIGHTS, i + 1))
        if done:
            wb[i + 1].update(zip(FFN1_WEIGHTS, done))
        states.append([nap[:, HIST_A_PAD - (CONV_A - 1):], nas[:, HIST_A_PAD - (CONV_A - 1):],
                       nbp[:, HIST_B_PAD - (CONV_B - 1):], nbs[:, HIST_B_PAD - (CONV_B - 1):]])
        chunk_v.append(vn)
    return (xp, xs, *[jnp.stack(s) for s in zip(*states)], jnp.stack(chunk_v))


def _front_pad(hist, pad_to):
    return jnp.pad(hist, ((0, 0), (0, 0), (pad_to - hist.shape[2], 0), (0, 0)))


def kernel(x_prompt, x_sample, p_prompt, p_sample, cache_conv_a, cache_conv_b, ffn1_norm, ffn1_w_gate, ffn1_w_up, ffn1_w_down, mix_norm, w_in, gate_bias, conv_a_w, conv_a_b, ln_a_g, ln_a_b, w_a_out, conv_b_w, w_b_out, ln_c_g, ln_c_b, w_spatial, b_spatial, w_c_out, w_o, ffn2_norm, ffn2_w_gate, ffn2_w_up, ffn2_w_down, ple_norm, w_ple_gate, w_ple_proj, final_norm):
    depth = w_in.shape[0]
    batch, seq, _ = x_prompt.shape
    dec_batch, dec_seq, _ = x_sample.shape
    row = lambda a: a.reshape(a.shape[0], 1, a.shape[1])
    w = dict(
        ffn1_norm=row(ffn1_norm), mix_norm=row(mix_norm), gate_bias=row(gate_bias),
        conv_a_w=conv_a_w, conv_a_b=row(conv_a_b), ln_a_g=row(ln_a_g), ln_a_b=row(ln_a_b),
        conv_b_w=conv_b_w, ln_c_g=row(ln_c_g), ln_c_b=row(ln_c_b), w_spatial=w_spatial,
        b_spatial=jnp.swapaxes(b_spatial, 1, 2), ffn2_norm=row(ffn2_norm), ple_norm=row(ple_norm),
        final_norm=final_norm.reshape(1, D_MODEL))
    raw = dict(ffn1_w_gate=ffn1_w_gate, ffn1_w_up=ffn1_w_up, ffn1_w_down=ffn1_w_down,
               w_in=w_in, w_a_out=w_a_out, w_b_out=w_b_out, w_c_out=w_c_out, w_o=w_o,
               ffn2_w_gate=ffn2_w_gate, ffn2_w_up=ffn2_w_up, ffn2_w_down=ffn2_w_down,
               w_ple_gate=w_ple_gate, w_ple_proj=w_ple_proj)

    (y_prompt, y_sample, conv_a_prompt, conv_a_sample, conv_b_prompt, conv_b_sample, chunk_v) = _trunk(
        x_prompt.reshape(batch * seq, D_MODEL), x_sample.reshape(dec_batch * dec_seq, D_MODEL),
        p_prompt.reshape(depth, batch * seq, D_PLE), p_sample.reshape(depth, dec_batch * dec_seq, D_PLE),
        jnp.zeros((depth, batch, HIST_A_PAD, W_BR), F32), _front_pad(cache_conv_a, HIST_A_PAD),
        jnp.zeros((depth, batch, HIST_B_PAD, W_BR), F32), _front_pad(cache_conv_b, HIST_B_PAD),
        w, raw, depth, seq, dec_seq)

    return (y_prompt.reshape(x_prompt.shape), y_sample.reshape(x_sample.shape),
            conv_a_prompt, conv_a_sample, conv_b_prompt, conv_b_sample,
            chunk_v.reshape(depth, dec_batch, dec_seq, W_BR))
```

```python
import functools
from typing import NamedTuple

import jax
import jax.numpy as jnp
from jax import lax
from jax.experimental import pallas as pl
from jax.experimental.pallas import tpu as pltpu

D_MODEL = 1024
D_FF = 2816
D_PLE = 256
W_BR = 512
CONV_A = 31
CONV_B = 3
H_C = 4
C_HEAD = W_BR // H_C
MLP_CHUNK = 128
EPS = 1e-6
COL_A, COL_B, COL_C, COL_G, COL_END = 0, 1024, 2560, 3584, 6656

HIST_A_PAD = 32
HIST_B_PAD = 8
SUBLANES = 8
BF16_ROWS = 16
LANES = 128
CONV_ACC_VREGS = 16
VMEM_LIMIT_BYTES = 56 * 1024 * 1024

BF16 = jnp.bfloat16
F32 = jnp.float32


def _dot(a, b):
    return jnp.dot(a, b, preferred_element_type=F32)


def _row_scale(x):
    return lax.rsqrt(jnp.mean(x * x, axis=-1, keepdims=True) + EPS)


def _rmsnorm(x, g):
    return x * _row_scale(x) * g


def _layernorm(x, g, b):
    mu = jnp.mean(x, axis=-1, keepdims=True)
    xc = x - mu
    return xc * lax.rsqrt(jnp.mean(xc * xc, axis=-1, keepdims=True) + EPS) * g + b


def _swiglu_half_step(x, g, wg_ref, wu_ref, wd_ref):
    r = _row_scale(x)
    xg = (x * g).astype(BF16)
    act = jax.nn.silu(r * _dot(xg, wg_ref[...])) * (r * _dot(xg, wu_ref[...]))
    return x + 0.5 * _dot(act.astype(BF16), wd_ref[...])


def _two_groups(n_prompt_steps, prompt_body, sample_body):
    step = pl.program_id(0)
    pl.when(step < n_prompt_steps)(prompt_body)
    pl.when(step == n_prompt_steps)(sample_body)


def _ffn_kernel(n_prompt_steps, xp_ref, xs_ref, g_ref, wg_ref, wu_ref, wd_ref, op_ref, os_ref):
    def tile(x_ref, o_ref):
        o_ref[...] = _swiglu_half_step(x_ref[...], g_ref[...], wg_ref, wu_ref, wd_ref)

    _two_groups(n_prompt_steps, functools.partial(tile, xp_ref, op_ref), functools.partial(tile, xs_ref, os_ref))


def _ffn_ple_kernel(final, n_prompt_steps, xp_ref, xs_ref, pp_ref, ps_ref, g_ref, wg_ref, wu_ref, wd_ref,
                    pg_ref, wpg_ref, wpp_ref, fg_ref, op_ref, os_ref):
    def tile(x_ref, p_ref, o_ref):
        x = _swiglu_half_step(x_ref[...], g_ref[...], wg_ref, wu_ref, wd_ref)
        gate = jax.nn.sigmoid(_row_scale(x) * _dot((x * pg_ref[...]).astype(BF16), wpg_ref[...]))
        x = x + gate * _dot(p_ref[...].astype(BF16), wpp_ref[...])
        if final:
            x = _rmsnorm(x, fg_ref[...])
        o_ref[...] = x

    _two_groups(n_prompt_steps, functools.partial(tile, xp_ref, pp_ref, op_ref),
                functools.partial(tile, xs_ref, ps_ref, os_ref))


def _causal_dwconv(zp_ref, zsh_ref, w_ref, n_taps, first_row, n_seq, n_rows):
    if zsh_ref is not None:
        span = zsh_ref.shape[2]
        for r in range(1, SUBLANES):
            zsh_ref[r - 1] = zp_ref[:, pl.ds(r, span), :]
    rows = min(n_rows, 32)
    seqs = max(1, min(n_seq, CONV_ACC_VREGS * SUBLANES * LANES // (rows * W_BR)))
    w = w_ref[...]
    seq_parts = []
    for s0 in range(0, n_seq, seqs):
        row_parts = []
        for r0 in range(0, n_rows, rows):
            acc = jnp.zeros((seqs, rows, W_BR), F32)
            for k in range(n_taps):
                shift, base = (first_row + k) % SUBLANES, (first_row + k) // SUBLANES * SUBLANES
                if zsh_ref is None:
                    win = zp_ref[pl.ds(s0, seqs), pl.ds(first_row + r0 + k, rows), :]
                elif shift == 0:
                    win = zp_ref[pl.ds(s0, seqs), pl.ds(base + r0, rows), :]
                else:
                    win = zsh_ref[shift - 1, pl.ds(s0, seqs), pl.ds(base + r0, rows), :]
                acc = acc + w[k] * win
            row_parts.append(acc)
        seq_parts.append(row_parts[0] if len(row_parts) == 1 else jnp.concatenate(row_parts, axis=1))
    return seq_parts[0] if len(seq_parts) == 1 else jnp.concatenate(seq_parts, axis=0)


def _spatial_mix(vn, wsp_ref, bsp_ref, n_seq, n_rows):
    t = min(n_rows, MLP_CHUNK)
    r_total = n_seq * n_rows
    tril = lax.broadcasted_iota(jnp.int32, (t, t), 0) >= lax.broadcasted_iota(jnp.int32, (t, t), 1)
    heads = []
    if t == MLP_CHUNK:
        n_chunks = r_total // t
        bias = bsp_ref[...]
        for h in range(H_C):
            cols = slice(h * C_HEAD, (h + 1) * C_HEAD)
            wm = jnp.where(tril, wsp_ref[h], 0.0).astype(BF16)
            rhs = jnp.concatenate([vn[j * t:(j + 1) * t, cols] for j in range(n_chunks)], axis=1)
            mh = _dot(wm, rhs) + bias[:, h:h + 1]
            heads.append(jnp.concatenate([mh[:, j * C_HEAD:(j + 1) * C_HEAD] for j in range(n_chunks)], axis=0))
    else:
        sel = (lax.broadcasted_iota(jnp.int32, (r_total, t), 0) % t
               == lax.broadcasted_iota(jnp.int32, (r_total, t), 1)).astype(F32)
        sel_t = (lax.broadcasted_iota(jnp.int32, (t, r_total), 1) % t
                 == lax.broadcasted_iota(jnp.int32, (t, r_total), 0)).astype(F32)
        same_seq = (lax.broadcasted_iota(jnp.int32, (r_total, r_total), 0) // t
                    == lax.broadcasted_iota(jnp.int32, (r_total, r_total), 1) // t)
        bias = jnp.concatenate([bsp_ref[0:t, :]] * n_seq, axis=0)
        for h in range(H_C):
            cols = slice(h * C_HEAD, (h + 1) * C_HEAD)
            wm = jnp.where(tril, wsp_ref[h, 0:t, 0:t], 0.0)
            big = jnp.where(same_seq, _dot(_dot(sel, wm), sel_t), 0.0).astype(BF16)
            heads.append(_dot(big, vn[:, cols]) + bias[:, h:h + 1])
    return jnp.concatenate(heads, axis=1)


def _mixer_tile(n_seq, n_rows, fresh, x_ref, hista_ref, histb_ref, params, o_ref, newa_ref, newb_ref,
                vn_ref, zpa_ref, zpb_ref, zsh_ref, h_ref, proj_ref):
    (g_ref, win_ref, gbias_ref, caw_ref, cab_ref, lnag_ref, lnab_ref, waout_ref, cbw_ref, wbout_ref,
     lncg_ref, lncb_ref, wsp_ref, bsp_ref, wcout_ref, wo_ref) = params
    r_total = n_seq * n_rows

    @pl.when(fresh)
    def _():
        zpa_ref[:, 0:HIST_A_PAD, :] = hista_ref[...]
        zpb_ref[:, 0:HIST_B_PAD, :] = histb_ref[...]

    h_ref[...] = _rmsnorm(x_ref[...], g_ref[...]).astype(BF16)

    proj_ref[:, COL_A:COL_B] = _dot(h_ref[...], win_ref[:, COL_A:COL_B])
    za = proj_ref[:, 0:W_BR] * jax.nn.sigmoid(proj_ref[:, W_BR:COL_B])
    zpa_ref[:, HIST_A_PAD:HIST_A_PAD + n_rows, :] = za.reshape(n_seq, n_rows, W_BR)

    proj_ref[:, COL_B:COL_G] = _dot(h_ref[...], win_ref[:, COL_B:COL_G])
    ca = _causal_dwconv(zpa_ref, zsh_ref, caw_ref, CONV_A, HIST_A_PAD - (CONV_A - 1), n_seq, n_rows)
    ca = ca.reshape(r_total, W_BR) + cab_ref[...]
    ya_in = jax.nn.silu(_layernorm(ca, lnag_ref[...], lnab_ref[...])).astype(BF16)
    tail_a = zpa_ref[:, n_rows:n_rows + HIST_A_PAD, :]
    zpa_ref[:, 0:HIST_A_PAD, :] = tail_a
    newa_ref[...] = tail_a

    proj_ref[:, COL_G:COL_END] = _dot(h_ref[...], win_ref[:, COL_G:COL_END])

    zb = proj_ref[:, COL_B + W_BR:COL_B + 2 * W_BR] * proj_ref[:, COL_B + 2 * W_BR:COL_C]
    zpb_ref[:, HIST_B_PAD:HIST_B_PAD + n_rows, :] = zb.reshape(n_seq, n_rows, W_BR)
    cb = _causal_dwconv(zpb_ref, None, cbw_ref, CONV_B, HIST_B_PAD - (CONV_B - 1), n_seq, n_rows)
    yb_in = (proj_ref[:, COL_B:COL_B + W_BR] * cb.reshape(r_total, W_BR)).astype(BF16)
    tail_b = zpb_ref[:, n_rows:n_rows + HIST_B_PAD, :]
    zpb_ref[:, 0:HIST_B_PAD, :] = tail_b
    newb_ref[...] = tail_b

    vn = _layernorm(jax.nn.gelu(proj_ref[:, COL_C + W_BR:COL_G]), lncg_ref[...], lncb_ref[...])
    if vn_ref is not None:
        vn_ref[...] = vn
    gu = jax.nn.gelu(proj_ref[:, COL_C:COL_C + W_BR])

    ya = _dot(ya_in, waout_ref[...])
    yb = _dot(yb_in, wbout_ref[...])
    mixed = _spatial_mix(vn.astype(BF16), wsp_ref, bsp_ref, n_seq, n_rows)
    yc = _dot((gu * mixed).astype(BF16), wcout_ref[...])

    gates = jax.nn.sigmoid(proj_ref[:, COL_G:COL_END] + gbias_ref[...])
    merged = (gates[:, :D_MODEL] * ya + gates[:, D_MODEL:2 * D_MODEL] * yb
              + gates[:, 2 * D_MODEL:] * yc)
    o_ref[...] = x_ref[...] + _dot(merged.astype(BF16), wo_ref[...])


N_MIXER_PARAMS = 16


def _mixer_kernel(n_prompt_steps, tiles_per_seq, prompt_rows, dec_batch, dec_seq,
                  xp_ref, xs_ref, hap_ref, has_ref, hbp_ref, hbs_ref, *refs):
    params = refs[:N_MIXER_PARAMS]
    (op_ref, os_ref, nap_ref, nas_ref, nbp_ref, nbs_ref, vn_ref,
     zpa_p, zpb_p, zsh_p, zpa_s, zpb_s, zsh_s, h_ref, proj_ref) = refs[N_MIXER_PARAMS:]
    step = pl.program_id(0)
    _two_groups(
        n_prompt_steps,
        lambda: _mixer_tile(1, prompt_rows, step % tiles_per_seq == 0, xp_ref, hap_ref, hbp_ref, params,
                            op_ref, nap_ref, nbp_ref, None, zpa_p, zpb_p, zsh_p, h_ref, proj_ref),
        lambda: _mixer_tile(dec_batch, dec_seq, step == n_prompt_steps, xs_ref, has_ref, hbs_ref, params,
                            os_ref, nas_ref, nbs_ref, vn_ref, zpa_s, zpb_s, zsh_s, h_ref, proj_ref))


class _Cast(NamedTuple):
    src: jax.Array
    layer: int


def _cast_chunks(rows, n_steps):
    units = rows // BF16_ROWS
    n_chunks = max(d for d in range(1, min(units, n_steps) + 1) if units % d == 0)
    return rows // n_chunks, n_chunks


def _with_casts(body, n_in, n_out, n_casts, *refs):
    ins, cast_ins = refs[:n_in], refs[n_in:n_in + n_casts]
    outs = refs[n_in + n_casts:n_in + n_casts + n_out]
    cast_outs = refs[n_in + n_casts + n_out:n_in + 2 * n_casts + n_out]
    body(*ins, *outs, *refs[n_in + 2 * n_casts + n_out:])
    for src_ref, dst_ref in zip(cast_ins, cast_outs):
        dst_ref[...] = src_ref[...].astype(BF16)


def _fused_call(body, name, grid, operands, in_specs, out_shapes, out_specs, scratch, casts):
    n_steps = 1
    for g in grid:
        n_steps *= g
    linear = (lambda t: t) if len(grid) == 1 else (lambda b, t: b * grid[1] + t)
    c_in, c_out, c_shapes = [], [], []
    for c in casts:
        _, rows, cols = c.src.shape
        chunk, n_chunks = _cast_chunks(rows, n_steps)
        at = lambda *g, n=n_chunks: jnp.minimum(linear(*g), n - 1)
        c_in.append(pl.BlockSpec((None, chunk, cols), lambda *g, at=at, layer=c.layer: (layer, at(*g), 0)))
        c_out.append(pl.BlockSpec((chunk, cols), lambda *g, at=at: (at(*g), 0)))
        c_shapes.append(jax.ShapeDtypeStruct((rows, cols), BF16))
    outs = pl.pallas_call(
        functools.partial(_with_casts, body, len(operands), len(out_shapes), len(casts)),
        out_shape=tuple(out_shapes) + tuple(c_shapes),
        grid=grid,
        in_specs=list(in_specs) + c_in,
        out_specs=tuple(out_specs) + tuple(c_out),
        scratch_shapes=scratch,
        compiler_params=pltpu.CompilerParams(dimension_semantics=("arbitrary",) * len(grid),
                                             vmem_limit_bytes=VMEM_LIMIT_BYTES),
        name=name,
    )(*operands, *[c.src for c in casts])
    return outs[:len(out_shapes)], outs[len(out_shapes):]


def _layer_spec(layer, tail_shape):
    zeros = (0,) * len(tail_shape)
    return pl.BlockSpec((None,) + tuple(tail_shape), lambda *g: (layer,) + zeros,
                        pipeline_mode=pl.Buffered(1))


def _resident_spec(shape):
    return pl.BlockSpec(tuple(shape), lambda *g: (0, 0), pipeline_mode=pl.Buffered(1))


def _tile_spec(rows, width, n_prompt_steps):
    return pl.BlockSpec((rows, width), lambda t: (jnp.minimum(t, n_prompt_steps - 1), 0))


def _whole_spec(shape):
    return pl.BlockSpec(tuple(shape), lambda t: (0,) * len(shape))


def _ffn_call(xp, xs, layer, w, wb, tm, casts):
    n_p = xp.shape[0] // tm
    outs, cast_out = _fused_call(
        functools.partial(_ffn_kernel, n_p), "ffn1", (n_p + 1,),
        [xp, xs, w['ffn1_norm'], wb['ffn1_w_gate'], wb['ffn1_w_up'], wb['ffn1_w_down']],
        [_tile_spec(tm, D_MODEL, n_p), _whole_spec(xs.shape), _layer_spec(layer, (1, D_MODEL)),
         _resident_spec((D_MODEL, D_FF)), _resident_spec((D_MODEL, D_FF)), _resident_spec((D_FF, D_MODEL))],
        [jax.ShapeDtypeStruct(xp.shape, F32), jax.ShapeDtypeStruct(xs.shape, F32)],
        [_tile_spec(tm, D_MODEL, n_p), _whole_spec(xs.shape)], [], casts)
    return outs, cast_out


def _ffn_ple_call(xp, xs, pp, ps, layer, w, wb, tm, final, casts):
    n_p = xp.shape[0] // tm
    outs, cast_out = _fused_call(
        functools.partial(_ffn_ple_kernel, final, n_p), "ffn2_ple", (n_p + 1,),
        [xp, xs, pp, ps, w['ffn2_norm'], wb['ffn2_w_gate'], wb['ffn2_w_up'], wb['ffn2_w_down'],
         w['ple_norm'], wb['w_ple_gate'], wb['w_ple_proj'], w['final_norm']],
        [_tile_spec(tm, D_MODEL, n_p), _whole_spec(xs.shape),
         pl.BlockSpec((None, tm, D_PLE), lambda t: (layer, jnp.minimum(t, n_p - 1), 0)),
         pl.BlockSpec((None,) + ps.shape[1:], lambda t: (layer, 0, 0)),
         _layer_spec(layer, (1, D_MODEL)), _resident_spec((D_MODEL, D_FF)),
         _resident_spec((D_MODEL, D_FF)), _resident_spec((D_FF, D_MODEL)),
         _layer_spec(layer, (1, D_MODEL)), _resident_spec((D_MODEL, D_MODEL)),
         _resident_spec((D_PLE, D_MODEL)), _resident_spec((1, D_MODEL))],
        [jax.ShapeDtypeStruct(xp.shape, F32), jax.ShapeDtypeStruct(xs.shape, F32)],
        [_tile_spec(tm, D_MODEL, n_p), _whole_spec(xs.shape)], [], casts)
    return outs, cast_out


def _mixer_call(xp, xs, hist_ap, hist_as, hist_bp, hist_bs, layer, w, wb, seq, dec_seq, tile_rows, casts):
    n_p = xp.shape[0] // tile_rows
    tiles_per_seq = seq // tile_rows
    dec_batch = xs.shape[0] // dec_seq
    assert xs.shape[0] == tile_rows
    ls = functools.partial(_layer_spec, layer)
    seq_of_step = lambda t: jnp.minimum(t, n_p - 1) // tiles_per_seq
    state_p = lambda pad: pl.BlockSpec((1, pad, W_BR), lambda t: (seq_of_step(t), 0, 0))
    state_s = lambda pad: _whole_spec((dec_batch, pad, W_BR))
    in_specs = [
        _tile_spec(tile_rows, D_MODEL, n_p), _whole_spec(xs.shape),
        state_p(HIST_A_PAD), state_s(HIST_A_PAD), state_p(HIST_B_PAD), state_s(HIST_B_PAD),
        ls((1, D_MODEL)), _resident_spec((D_MODEL, COL_END)), ls((1, COL_END - COL_G)),
        ls((CONV_A, W_BR)), ls((1, W_BR)), ls((1, W_BR)), ls((1, W_BR)), _resident_spec((W_BR, D_MODEL)),
        ls((CONV_B, W_BR)), _resident_spec((W_BR, D_MODEL)),
        ls((1, W_BR)), ls((1, W_BR)), ls((H_C, MLP_CHUNK, MLP_CHUNK)), ls((MLP_CHUNK, H_C)),
        _resident_spec((W_BR, D_MODEL)), _resident_spec((D_MODEL, D_MODEL)),
    ]
    f32 = lambda *shape: jax.ShapeDtypeStruct(shape, F32)
    out_shapes = [f32(*xp.shape), f32(*xs.shape),
                  f32(hist_ap.shape[0], HIST_A_PAD, W_BR), f32(dec_batch, HIST_A_PAD, W_BR),
                  f32(hist_bp.shape[0], HIST_B_PAD, W_BR), f32(dec_batch, HIST_B_PAD, W_BR),
                  f32(xs.shape[0], W_BR)]
    out_specs = [_tile_spec(tile_rows, D_MODEL, n_p), _whole_spec(xs.shape),
                 state_p(HIST_A_PAD), state_s(HIST_A_PAD), state_p(HIST_B_PAD), state_s(HIST_B_PAD),
                 _whole_spec((xs.shape[0], W_BR))]

    def conv_scratch(n_seq, n_rows):
        return [pltpu.VMEM((n_seq, HIST_A_PAD + n_rows, W_BR), F32),
                pltpu.VMEM((n_seq, HIST_B_PAD + n_rows, W_BR), F32),
                pltpu.VMEM((SUBLANES - 1, n_seq, HIST_A_PAD - SUBLANES + n_rows, W_BR), F32)]

    scratch = (conv_scratch(1, tile_rows) + conv_scratch(dec_batch, dec_seq)
               + [pltpu.VMEM((tile_rows, D_MODEL), BF16), pltpu.VMEM((tile_rows, COL_END), F32)])
    params = [w['mix_norm'], wb['w_in'], w['gate_bias'],
              w['conv_a_w'], w['conv_a_b'], w['ln_a_g'], w['ln_a_b'], wb['w_a_out'],
              w['conv_b_w'], wb['w_b_out'], w['ln_c_g'], w['ln_c_b'], w['w_spatial'], w['b_spatial'],
              wb['w_c_out'], wb['w_o']]
    assert len(params) == N_MIXER_PARAMS
    return _fused_call(
        functools.partial(_mixer_kernel, n_p, tiles_per_seq, tile_rows, dec_batch, dec_seq), "mixer",
        (n_p + 1,), [xp, xs, hist_ap, hist_as, hist_bp, hist_bs] + params,
        in_specs, out_shapes, out_specs, scratch, casts)


FFN1_WEIGHTS = ('ffn1_w_gate', 'ffn1_w_up', 'ffn1_w_down')
MIXER_WEIGHTS = ('w_in', 'w_a_out', 'w_b_out', 'w_c_out', 'w_o')
FFN2_WEIGHTS = ('ffn2_w_gate', 'ffn2_w_up', 'ffn2_w_down', 'w_ple_gate', 'w_ple_proj')
CAST_ONLY_STEPS = 16


MIXER_TILE_ROWS = 256
FFN_TILE_ROWS = 512


def _trunk(xp, xs, pp, ps, hist_ap, hist_as, hist_bp, hist_bs, w, raw, depth, seq, dec_seq):
    casts = lambda names, layer: [_Cast(raw[n], layer) for n in names] if layer < depth else []
    wb = [dict() for _ in range(depth)]
    _, done = _fused_call(lambda: None, "cast", (CAST_ONLY_STEPS,), [], [], [], [], [], casts(FFN1_WEIGHTS, 0))
    wb[0].update(zip(FFN1_WEIGHTS, done))
    states, chunk_v = [], []
    for i in range(depth):
        (xp, xs), done = _ffn_call(xp, xs, i, w, wb[i], FFN_TILE_ROWS, casts(MIXER_WEIGHTS, i))
        wb[i].update(zip(MIXER_WEIGHTS, done))
        (xp, xs, nap, nas, nbp, nbs, vn), done = _mixer_call(
            xp, xs, hist_ap[i], hist_as[i], hist_bp[i], hist_bs[i], i, w, wb[i], seq, dec_seq,
            MIXER_TILE_ROWS, casts(FFN2_WEIGHTS, i))
        wb[i].update(zip(FFN2_WEIGHTS, done))
        (xp, xs), done = _ffn_ple_call(xp, xs, pp, ps, i, w, wb[i], FFN_TILE_ROWS, i == depth - 1,
                                       casts(FFN1_WEIGHTS, i + 1))
        if done:
            wb[i + 1].update(zip(FFN1_WEIGHTS, done))
        states.append([nap[:, HIST_A_PAD - (CONV_A - 1):], nas[:, HIST_A_PAD - (CONV_A - 1):],
                       nbp[:, HIST_B_PAD - (CONV_B - 1):], nbs[:, HIST_B_PAD - (CONV_B - 1):]])
        chunk_v.append(vn)
    return (xp, xs, *[jnp.stack(s) for s in zip(*states)], jnp.stack(chunk_v))


def _front_pad(hist, pad_to):
    return jnp.pad(hist, ((0, 0), (0, 0), (pad_to - hist.shape[2], 0), (0, 0)))


def kernel(x_prompt, x_sample, p_prompt, p_sample, cache_conv_a, cache_conv_b, ffn1_norm, ffn1_w_gate, ffn1_w_up, ffn1_w_down, mix_norm, w_in, gate_bias, conv_a_w, conv_a_b, ln_a_g, ln_a_b, w_a_out, conv_b_w, w_b_out, ln_c_g, ln_c_b, w_spatial, b_spatial, w_c_out, w_o, ffn2_norm, ffn2_w_gate, ffn2_w_up, ffn2_w_down, ple_norm, w_ple_gate, w_ple_proj, final_norm):
    depth = w_in.shape[0]
    batch, seq, _ = x_prompt.shape
    dec_batch, dec_seq, _ = x_sample.shape
    row = lambda a: a.reshape(a.shape[0], 1, a.shape[1])
    w = dict(
        ffn1_norm=row(ffn1_norm), mix_norm=row(mix_norm), gate_bias=row(gate_bias),
        conv_a_w=conv_a_w, conv_a_b=row(conv_a_b), ln_a_g=row(ln_a_g), ln_a_b=row(ln_a_b),
        conv_b_w=conv_b_w, ln_c_g=row(ln_c_g), ln_c_b=row(ln_c_b), w_spatial=w_spatial,
        b_spatial=jnp.swapaxes(b_spatial, 1, 2), ffn2_norm=row(ffn2_norm), ple_norm=row(ple_norm),
        final_norm=final_norm.reshape(1, D_MODEL))
    raw = dict(ffn1_w_gate=ffn1_w_gate, ffn1_w_up=ffn1_w_up, ffn1_w_down=ffn1_w_down,
               w_in=w_in, w_a_out=w_a_out, w_b_out=w_b_out, w_c_out=w_c_out, w_o=w_o,
               ffn2_w_gate=ffn2_w_gate, ffn2_w_up=ffn2_w_up, ffn2_w_down=ffn2_w_down,
               w_ple_gate=w_ple_gate, w_ple_proj=w_ple_proj)

    (y_prompt, y_sample, conv_a_prompt, conv_a_sample, conv_b_prompt, conv_b_sample, chunk_v) = _trunk(
        x_prompt.reshape(batch * seq, D_MODEL), x_sample.reshape(dec_batch * dec_seq, D_MODEL),
        p_prompt.reshape(depth, batch * seq, D_PLE), p_sample.reshape(depth, dec_batch * dec_seq, D_PLE),
        jnp.zeros((depth, batch, HIST_A_PAD, W_BR), F32), _front_pad(cache_conv_a, HIST_A_PAD),
        jnp.zeros((depth, batch, HIST_B_PAD, W_BR), F32), _front_pad(cache_conv_b, HIST_B_PAD),
        w, raw, depth, seq, dec_seq)

    return (y_prompt.reshape(x_prompt.shape), y_sample.reshape(x_sample.shape),
            conv_a_prompt, conv_a_sample, conv_b_prompt, conv_b_sample,
            chunk_v.reshape(depth, dec_batch, dec_seq, W_BR))
```

```python
import functools
from typing import NamedTuple

import jax
import jax.numpy as jnp
from jax import lax
from jax.experimental import pallas as pl
from jax.experimental.pallas import tpu as pltpu

D_MODEL = 1024
D_FF = 2816
D_PLE = 256
W_BR = 512
CONV_A = 31
CONV_B = 3
H_C = 4
C_HEAD = W_BR // H_C
MLP_CHUNK = 128
EPS = 1e-6
COL_A, COL_B, COL_C, COL_G, COL_END = 0, 1024, 2560, 3584, 6656

HIST_A_PAD = 32
HIST_B_PAD = 8
SUBLANES = 8
BF16_ROWS = 16
LANES = 128
CONV_ACC_VREGS = 16
VMEM_LIMIT_BYTES = 56 * 1024 * 1024

BF16 = jnp.bfloat16
F32 = jnp.float32


def _dot(a, b):
    return jnp.dot(a, b, preferred_element_type=F32)


def _rmsnorm(x, g):
    return x * lax.rsqrt(jnp.mean(x * x, axis=-1, keepdims=True) + EPS) * g


def _layernorm(x, g, b):
    mu = jnp.mean(x, axis=-1, keepdims=True)
    xc = x - mu
    return xc * lax.rsqrt(jnp.mean(xc * xc, axis=-1, keepdims=True) + EPS) * g + b


def _swiglu_half_step(x, g, wg_ref, wu_ref, wd_ref):
    h = _rmsnorm(x, g).astype(BF16)
    act = jax.nn.silu(_dot(h, wg_ref[...])) * _dot(h, wu_ref[...])
    return x + 0.5 * _dot(act.astype(BF16), wd_ref[...])


def _two_groups(n_prompt_steps, prompt_body, sample_body):
    step = pl.program_id(0)
    pl.when(step < n_prompt_steps)(prompt_body)
    pl.when(step == n_prompt_steps)(sample_body)


def _ffn_kernel(n_prompt_steps, xp_ref, xs_ref, g_ref, wg_ref, wu_ref, wd_ref, op_ref, os_ref):
    def tile(x_ref, o_ref):
        o_ref[...] = _swiglu_half_step(x_ref[...], g_ref[...], wg_ref, wu_ref, wd_ref)

    _two_groups(n_prompt_steps, functools.partial(tile, xp_ref, op_ref), functools.partial(tile, xs_ref, os_ref))


def _ffn_ple_kernel(final, n_prompt_steps, xp_ref, xs_ref, pp_ref, ps_ref, g_ref, wg_ref, wu_ref, wd_ref,
                    pg_ref, wpg_ref, wpp_ref, fg_ref, op_ref, os_ref):
    def tile(x_ref, p_ref, o_ref):
        x = _swiglu_half_step(x_ref[...], g_ref[...], wg_ref, wu_ref, wd_ref)
        gate = jax.nn.sigmoid(_dot(_rmsnorm(x, pg_ref[...]).astype(BF16), wpg_ref[...]))
        x = x + gate * _dot(p_ref[...].astype(BF16), wpp_ref[...])
        if final:
            x = _rmsnorm(x, fg_ref[...])
        o_ref[...] = x

    _two_groups(n_prompt_steps, functools.partial(tile, xp_ref, pp_ref, op_ref),
                functools.partial(tile, xs_ref, ps_ref, os_ref))


def _causal_dwconv(zp_ref, zsh_ref, w_ref, n_taps, first_row, n_seq, n_rows):
    if zsh_ref is not None:
        span = zsh_ref.shape[2]
        for r in range(1, SUBLANES):
            zsh_ref[r - 1] = zp_ref[:, pl.ds(r, span), :]
    rows = min(n_rows, 32)
    seqs = max(1, min(n_seq, CONV_ACC_VREGS * SUBLANES * LANES // (rows * W_BR)))
    w = w_ref[...]
    seq_parts = []
    for s0 in range(0, n_seq, seqs):
        row_parts = []
        for r0 in range(0, n_rows, rows):
            acc = jnp.zeros((seqs, rows, W_BR), F32)
            for k in range(n_taps):
                shift, base = (first_row + k) % SUBLANES, (first_row + k) // SUBLANES * SUBLANES
                if zsh_ref is None:
                    win = zp_ref[pl.ds(s0, seqs), pl.ds(first_row + r0 + k, rows), :]
                elif shift == 0:
                    win = zp_ref[pl.ds(s0, seqs), pl.ds(base + r0, rows), :]
                else:
                    win = zsh_ref[shift - 1, pl.ds(s0, seqs), pl.ds(base + r0, rows), :]
                acc = acc + w[k] * win
            row_parts.append(acc)
        seq_parts.append(row_parts[0] if len(row_parts) == 1 else jnp.concatenate(row_parts, axis=1))
    return seq_parts[0] if len(seq_parts) == 1 else jnp.concatenate(seq_parts, axis=0)


def _spatial_mix(vn, wsp_ref, bsp_ref, n_seq, n_rows):
    t = min(n_rows, MLP_CHUNK)
    r_total = n_seq * n_rows
    tril = lax.broadcasted_iota(jnp.int32, (t, t), 0) >= lax.broadcasted_iota(jnp.int32, (t, t), 1)
    heads = []
    if t == MLP_CHUNK:
        n_chunks = r_total // t
        bias = bsp_ref[...]
        for h in range(H_C):
            cols = slice(h * C_HEAD, (h + 1) * C_HEAD)
            wm = jnp.where(tril, wsp_ref[h], 0.0).astype(BF16)
            rhs = jnp.concatenate([vn[j * t:(j + 1) * t, cols] for j in range(n_chunks)], axis=1)
            mh = _dot(wm, rhs) + bias[:, h:h + 1]
            heads.append(jnp.concatenate([mh[:, j * C_HEAD:(j + 1) * C_HEAD] for j in range(n_chunks)], axis=0))
    else:
        sel = (lax.broadcasted_iota(jnp.int32, (r_total, t), 0) % t
               == lax.broadcasted_iota(jnp.int32, (r_total, t), 1)).astype(F32)
        sel_t = (lax.broadcasted_iota(jnp.int32, (t, r_total), 1) % t
                 == lax.broadcasted_iota(jnp.int32, (t, r_total), 0)).astype(F32)
        same_seq = (lax.broadcasted_iota(jnp.int32, (r_total, r_total), 0) // t
                    == lax.broadcasted_iota(jnp.int32, (r_total, r_total), 1) // t)
        bias = jnp.concatenate([bsp_ref[0:t, :]] * n_seq, axis=0)
        for h in range(H_C):
            cols = slice(h * C_HEAD, (h + 1) * C_HEAD)
            wm = jnp.where(tril, wsp_ref[h, 0:t, 0:t], 0.0)
            big = jnp.where(same_seq, _dot(_dot(sel, wm), sel_t), 0.0).astype(BF16)
            heads.append(_dot(big, vn[:, cols]) + bias[:, h:h + 1])
    return jnp.concatenate(heads, axis=1)


def _mixer_tile(n_seq, n_rows, fresh, x_ref, hista_ref, histb_ref, params, o_ref, newa_ref, newb_ref,
                vn_ref, zpa_ref, zpb_ref, zsh_ref, h_ref, proj_ref):
    (g_ref, win_ref, gbias_ref, caw_ref, cab_ref, lnag_ref, lnab_ref, waout_ref, cbw_ref, wbout_ref,
     lncg_ref, lncb_ref, wsp_ref, bsp_ref, wcout_ref, wo_ref) = params
    r_total = n_seq * n_rows

    @pl.when(fresh)
    def _():
        zpa_ref[:, 0:HIST_A_PAD, :] = hista_ref[...]
        zpb_ref[:, 0:HIST_B_PAD, :] = histb_ref[...]

    h_ref[...] = _rmsnorm(x_ref[...], g_ref[...]).astype(BF16)

    proj_ref[:, COL_A:COL_B] = _dot(h_ref[...], win_ref[:, COL_A:COL_B])
    za = proj_ref[:, 0:W_BR] * jax.nn.sigmoid(proj_ref[:, W_BR:COL_B])
    zpa_ref[:, HIST_A_PAD:HIST_A_PAD + n_rows, :] = za.reshape(n_seq, n_rows, W_BR)

    proj_ref[:, COL_B:COL_G] = _dot(h_ref[...], win_ref[:, COL_B:COL_G])
    ca = _causal_dwconv(zpa_ref, zsh_ref, caw_ref, CONV_A, HIST_A_PAD - (CONV_A - 1), n_seq, n_rows)
    ca = ca.reshape(r_total, W_BR) + cab_ref[...]
    ya_in = jax.nn.silu(_layernorm(ca, lnag_ref[...], lnab_ref[...])).astype(BF16)
    tail_a = zpa_ref[:, n_rows:n_rows + HIST_A_PAD, :]
    zpa_ref[:, 0:HIST_A_PAD, :] = tail_a
    newa_ref[...] = tail_a

    proj_ref[:, COL_G:COL_END] = _dot(h_ref[...], win_ref[:, COL_G:COL_END])

    zb = proj_ref[:, COL_B + W_BR:COL_B + 2 * W_BR] * proj_ref[:, COL_B + 2 * W_BR:COL_C]
    zpb_ref[:, HIST_B_PAD:HIST_B_PAD + n_rows, :] = zb.reshape(n_seq, n_rows, W_BR)
    cb = _causal_dwconv(zpb_ref, None, cbw_ref, CONV_B, HIST_B_PAD - (CONV_B - 1), n_seq, n_rows)
    yb_in = (proj_ref[:, COL_B:COL_B + W_BR] * cb.reshape(r_total, W_BR)).astype(BF16)
    tail_b = zpb_ref[:, n_rows:n_rows + HIST_B_PAD, :]
    zpb_ref[:, 0:HIST_B_PAD, :] = tail_b
    newb_ref[...] = tail_b

    vn = _layernorm(jax.nn.gelu(proj_ref[:, COL_C + W_BR:COL_G]), lncg_ref[...], lncb_ref[...])
    if vn_ref is not None:
        vn_ref[...] = vn
    gu = jax.nn.gelu(proj_ref[:, COL_C:COL_C + W_BR])

    ya = _dot(ya_in, waout_ref[...])
    yb = _dot(yb_in, wbout_ref[...])
    mixed = _spatial_mix(vn.astype(BF16), wsp_ref, bsp_ref, n_seq, n_rows)
    yc = _dot((gu * mixed).astype(BF16), wcout_ref[...])

    gates = jax.nn.sigmoid(proj_ref[:, COL_G:COL_END] + gbias_ref[...])
    merged = (gates[:, :D_MODEL] * ya + gates[:, D_MODEL:2 * D_MODEL] * yb
              + gates[:, 2 * D_MODEL:] * yc)
    o_ref[...] = x_ref[...] + _dot(merged.astype(BF16), wo_ref[...])


N_MIXER_PARAMS = 16


def _mixer_kernel(n_prompt_steps, tiles_per_seq, prompt_rows, dec_batch, dec_seq,
                  xp_ref, xs_ref, hap_ref, has_ref, hbp_ref, hbs_ref, *refs):
    params = refs[:N_MIXER_PARAMS]
    (op_ref, os_ref, nap_ref, nas_ref, nbp_ref, nbs_ref, vn_ref,
     zpa_p, zpb_p, zsh_p, h_ref, proj_ref, zpa_s, zpb_s, zsh_s) = refs[N_MIXER_PARAMS:]
    step = pl.program_id(0)
    _two_groups(
        n_prompt_steps,
        lambda: _mixer_tile(1, prompt_rows, step % tiles_per_seq == 0, xp_ref, hap_ref, hbp_ref, params,
                            op_ref, nap_ref, nbp_ref, None, zpa_p, zpb_p, zsh_p, h_ref, proj_ref),
        lambda: _mixer_tile(dec_batch, dec_seq, step == n_prompt_steps, xs_ref, has_ref, hbs_ref, params,
                            os_ref, nas_ref, nbs_ref, vn_ref, zpa_s, zpb_s, zsh_s, h_ref, proj_ref))


class _Cast(NamedTuple):
    src: jax.Array
    layer: int


def _cast_chunks(rows, n_steps):
    units = rows // BF16_ROWS
    n_chunks = max(d for d in range(1, min(units, n_steps) + 1) if units % d == 0)
    return rows // n_chunks, n_chunks


def _with_casts(body, n_in, n_out, n_casts, *refs):
    ins, cast_ins = refs[:n_in], refs[n_in:n_in + n_casts]
    outs = refs[n_in + n_casts:n_in + n_casts + n_out]
    cast_outs = refs[n_in + n_casts + n_out:n_in + 2 * n_casts + n_out]
    body(*ins, *outs, *refs[n_in + 2 * n_casts + n_out:])
    for src_ref, dst_ref in zip(cast_ins, cast_outs):
        dst_ref[...] = src_ref[...].astype(BF16)


def _fused_call(body, name, grid, operands, in_specs, out_shapes, out_specs, scratch, casts):
    n_steps = 1
    for g in grid:
        n_steps *= g
    linear = (lambda t: t) if len(grid) == 1 else (lambda b, t: b * grid[1] + t)
    c_in, c_out, c_shapes = [], [], []
    for c in casts:
        _, rows, cols = c.src.shape
        chunk, n_chunks = _cast_chunks(rows, n_steps)
        at = lambda *g, n=n_chunks: jnp.minimum(linear(*g), n - 1)
        c_in.append(pl.BlockSpec((None, chunk, cols), lambda *g, at=at, layer=c.layer: (layer, at(*g), 0)))
        c_out.append(pl.BlockSpec((chunk, cols), lambda *g, at=at: (at(*g), 0)))
        c_shapes.append(jax.ShapeDtypeStruct((rows, cols), BF16))
    outs = pl.pallas_call(
        functools.partial(_with_casts, body, len(operands), len(out_shapes), len(casts)),
        out_shape=tuple(out_shapes) + tuple(c_shapes),
        grid=grid,
        in_specs=list(in_specs) + c_in,
        out_specs=tuple(out_specs) + tuple(c_out),
        scratch_shapes=scratch,
        compiler_params=pltpu.CompilerParams(dimension_semantics=("arbitrary",) * len(grid),
                                             vmem_limit_bytes=VMEM_LIMIT_BYTES),
        name=name,
    )(*operands, *[c.src for c in casts])
    return outs[:len(out_shapes)], outs[len(out_shapes):]


def _layer_spec(layer, tail_shape):
    zeros = (0,) * len(tail_shape)
    return pl.BlockSpec((None,) + tuple(tail_shape), lambda *g: (layer,) + zeros,
                        pipeline_mode=pl.Buffered(1))


def _resident_spec(shape):
    return pl.BlockSpec(tuple(shape), lambda *g: (0, 0), pipeline_mode=pl.Buffered(1))


def _tile_spec(rows, width, n_prompt_steps):
    return pl.BlockSpec((rows, width), lambda t: (jnp.minimum(t, n_prompt_steps - 1), 0))


def _whole_spec(shape):
    return pl.BlockSpec(tuple(shape), lambda t: (0,) * len(shape))


def _ffn_call(xp, xs, layer, w, wb, tm, casts):
    n_p = xp.shape[0] // tm
    outs, cast_out = _fused_call(
        functools.partial(_ffn_kernel, n_p), "ffn1", (n_p + 1,),
        [xp, xs, w['ffn1_norm'], wb['ffn1_w_gate'], wb['ffn1_w_up'], wb['ffn1_w_down']],
        [_tile_spec(tm, D_MODEL, n_p), _whole_spec(xs.shape), _layer_spec(layer, (1, D_MODEL)),
         _resident_spec((D_MODEL, D_FF)), _resident_spec((D_MODEL, D_FF)), _resident_spec((D_FF, D_MODEL))],
        [jax.ShapeDtypeStruct(xp.shape, F32), jax.ShapeDtypeStruct(xs.shape, F32)],
        [_tile_spec(tm, D_MODEL, n_p), _whole_spec(xs.shape)], [], casts)
    return outs, cast_out


def _ffn_ple_call(xp, xs, pp, ps, layer, w, wb, tm, final, casts):
    n_p = xp.shape[0] // tm
    outs, cast_out = _fused_call(
        functools.partial(_ffn_ple_kernel, final, n_p), "ffn2_ple", (n_p + 1,),
        [xp, xs, pp, ps, w['ffn2_norm'], wb['ffn2_w_gate'], wb['ffn2_w_up'], wb['ffn2_w_down'],
         w['ple_norm'], wb['w_ple_gate'], wb['w_ple_proj'], w['final_norm']],
        [_tile_spec(tm, D_MODEL, n_p), _whole_spec(xs.shape),
         pl.BlockSpec((None, tm, D_PLE), lambda t: (layer, jnp.minimum(t, n_p - 1), 0)),
         pl.BlockSpec((None,) + ps.shape[1:], lambda t: (layer, 0, 0)),
         _layer_spec(layer, (1, D_MODEL)), _resident_spec((D_MODEL, D_FF)),
         _resident_spec((D_MODEL, D_FF)), _resident_spec((D_FF, D_MODEL)),
         _layer_spec(layer, (1, D_MODEL)), _resident_spec((D_MODEL, D_MODEL)),
         _resident_spec((D_PLE, D_MODEL)), _resident_spec((1, D_MODEL))],
        [jax.ShapeDtypeStruct(xp.shape, F32), jax.ShapeDtypeStruct(xs.shape, F32)],
        [_tile_spec(tm, D_MODEL, n_p), _whole_spec(xs.shape)], [], casts)
    return outs, cast_out


def _mixer_call(xp, xs, hist_ap, hist_as, hist_bp, hist_bs, layer, w, wb, seq, dec_seq, tile_rows, casts):
    n_p = xp.shape[0] // tile_rows
    tiles_per_seq = seq // tile_rows
    dec_batch = xs.shape[0] // dec_seq
    assert xs.shape[0] == tile_rows
    ls = functools.partial(_layer_spec, layer)
    seq_of_step = lambda t: jnp.minimum(t, n_p - 1) // tiles_per_seq
    state_p = lambda pad: pl.BlockSpec((1, pad, W_BR), lambda t: (seq_of_step(t), 0, 0))
    state_s = lambda pad: _whole_spec((dec_batch, pad, W_BR))
    in_specs = [
        _tile_spec(tile_rows, D_MODEL, n_p), _whole_spec(xs.shape),
        state_p(HIST_A_PAD), state_s(HIST_A_PAD), state_p(HIST_B_PAD), state_s(HIST_B_PAD),
        ls((1, D_MODEL)), _resident_spec((D_MODEL, COL_END)), ls((1, COL_END - COL_G)),
        ls((CONV_A, W_BR)), ls((1, W_BR)), ls((1, W_BR)), ls((1, W_BR)), _resident_spec((W_BR, D_MODEL)),
        ls((CONV_B, W_BR)), _resident_spec((W_BR, D_MODEL)),
        ls((1, W_BR)), ls((1, W_BR)), ls((H_C, MLP_CHUNK, MLP_CHUNK)), ls((MLP_CHUNK, H_C)),
        _resident_spec((W_BR, D_MODEL)), _resident_spec((D_MODEL, D_MODEL)),
    ]
    f32 = lambda *shape: jax.ShapeDtypeStruct(shape, F32)
    out_shapes = [f32(*xp.shape), f32(*xs.shape),
                  f32(hist_ap.shape[0], HIST_A_PAD, W_BR), f32(dec_batch, HIST_A_PAD, W_BR),
                  f32(hist_bp.shape[0], HIST_B_PAD, W_BR), f32(dec_batch, HIST_B_PAD, W_BR),
                  f32(xs.shape[0], W_BR)]
    out_specs = [_tile_spec(tile_rows, D_MODEL, n_p), _whole_spec(xs.shape),
                 state_p(HIST_A_PAD), state_s(HIST_A_PAD), state_p(HIST_B_PAD), state_s(HIST_B_PAD),
                 _whole_spec((xs.shape[0], W_BR))]

    def conv_scratch(n_seq, n_rows):
        return [pltpu.VMEM((n_seq, HIST_A_PAD + n_rows, W_BR), F32),
                pltpu.VMEM((n_seq, HIST_B_PAD + n_rows, W_BR), F32),
                pltpu.VMEM((SUBLANES - 1, n_seq, HIST_A_PAD - SUBLANES + n_rows, W_BR), F32)]

    scratch = (conv_scratch(1, tile_rows)
               + [pltpu.VMEM((tile_rows, D_MODEL), BF16), pltpu.VMEM((tile_rows, COL_END), F32)]
               + conv_scratch(dec_batch, dec_seq))
    params = [w['mix_norm'], wb['w_in'], w['gate_bias'],
              w['conv_a_w'], w['conv_a_b'], w['ln_a_g'], w['ln_a_b'], wb['w_a_out'],
              w['conv_b_w'], wb['w_b_out'], w['ln_c_g'], w['ln_c_b'], w['w_spatial'], w['b_spatial'],
              wb['w_c_out'], wb['w_o']]
    assert len(params) == N_MIXER_PARAMS
    return _fused_call(
        functools.partial(_mixer_kernel, n_p, tiles_per_seq, tile_rows, dec_batch, dec_seq), "mixer",
        (n_p + 1,), [xp, xs, hist_ap, hist_as, hist_bp, hist_bs] + params,
        in_specs, out_shapes, out_specs, scratch, casts)


FFN1_WEIGHTS = ('ffn1_w_gate', 'ffn1_w_up', 'ffn1_w_down')
MIXER_WEIGHTS = ('w_in', 'w_a_out', 'w_b_out', 'w_c_out', 'w_o')
FFN2_WEIGHTS = ('ffn2_w_gate', 'ffn2_w_up', 'ffn2_w_down', 'w_ple_gate', 'w_ple_proj')
CAST_ONLY_STEPS = 16


MIXER_TILE_ROWS = 256
FFN_TILE_ROWS = 512


def _trunk(xp, xs, pp, ps, hist_ap, hist_as, hist_bp, hist_bs, w, raw, depth, seq, dec_seq):
    casts = lambda names, layer: [_Cast(raw[n], layer) for n in names] if layer < depth else []
    wb = [dict() for _ in range(depth)]
    _, done = _fused_call(lambda: None, "cast", (CAST_ONLY_STEPS,), [], [], [], [], [], casts(FFN1_WEIGHTS, 0))
    wb[0].update(zip(FFN1_WEIGHTS, done))
    states, chunk_v = [], []
    for i in range(depth):
        (xp, xs), done = _ffn_call(xp, xs, i, w, wb[i], FFN_TILE_ROWS, casts(MIXER_WEIGHTS, i))
        wb[i].update(zip(MIXER_WEIGHTS, done))
        (xp, xs, nap, nas, nbp, nbs, vn), done = _mixer_call(
            xp, xs, hist_ap[i], hist_as[i], hist_bp[i], hist_bs[i], i, w, wb[i], seq, dec_seq,
            MIXER_TILE_ROWS, casts(FFN2_WEIGHTS, i))
        wb[i].update(zip(FFN2_WEIGHTS, done))
        (xp, xs), done = _ffn_ple_call(xp, xs, pp, ps, i, w, wb[i], FFN_TILE_ROWS, i == depth - 1,
                                       casts(FFN1_WEIGHTS, i + 1))
        if done:
            wb[i + 1].update(zip(FFN1_WEIGHTS, done))
        states.append([nap[:, HIST_A_PAD - (CONV_A - 1):], nas[:, HIST_A_PAD - (CONV_A - 1):],
                       nbp[:, HIST_B_PAD - (CONV_B - 1):], nbs[:, HIST_B_PAD - (CONV_B - 1):]])
        chunk_v.append(vn)
    return (xp, xs, *[jnp.stack(s) for s in zip(*states)], jnp.stack(chunk_v))


def _front_pad(hist, pad_to):
    return jnp.pad(hist, ((0, 0), (0, 0), (pad_to - hist.shape[2], 0), (0, 0)))


def kernel(x_prompt, x_sample, p_prompt, p_sample, cache_conv_a, cache_conv_b, ffn1_norm, ffn1_w_gate, ffn1_w_up, ffn1_w_down, mix_norm, w_in, gate_bias, conv_a_w, conv_a_b, ln_a_g, ln_a_b, w_a_out, conv_b_w, w_b_out, ln_c_g, ln_c_b, w_spatial, b_spatial, w_c_out, w_o, ffn2_norm, ffn2_w_gate, ffn2_w_up, ffn2_w_down, ple_norm, w_ple_gate, w_ple_proj, final_norm):
    depth = w_in.shape[0]
    batch, seq, _ = x_prompt.shape
    dec_batch, dec_seq, _ = x_sample.shape
    row = lambda a: a.reshape(a.shape[0], 1, a.shape[1])
    w = dict(
        ffn1_norm=row(ffn1_norm), mix_norm=row(mix_norm), gate_bias=row(gate_bias),
        conv_a_w=conv_a_w, conv_a_b=row(conv_a_b), ln_a_g=row(ln_a_g), ln_a_b=row(ln_a_b),
        conv_b_w=conv_b_w, ln_c_g=row(ln_c_g), ln_c_b=row(ln_c_b), w_spatial=w_spatial,
        b_spatial=jnp.swapaxes(b_spatial, 1, 2), ffn2_norm=row(ffn2_norm), ple_norm=row(ple_norm),
        final_norm=final_norm.reshape(1, D_MODEL))
    raw = dict(ffn1_w_gate=ffn1_w_gate, ffn1_w_up=ffn1_w_up, ffn1_w_down=ffn1_w_down,
               w_in=w_in, w_a_out=w_a_out, w_b_out=w_b_out, w_c_out=w_c_out, w_o=w_o,
               ffn2_w_gate=ffn2_w_gate, ffn2_w_up=ffn2_w_up, ffn2_w_down=ffn2_w_down,
               w_ple_gate=w_ple_gate, w_ple_proj=w_ple_proj)

    (y_prompt, y_sample, conv_a_prompt, conv_a_sample, conv_b_prompt, conv_b_sample, chunk_v) = _trunk(
        x_prompt.reshape(batch * seq, D_MODEL), x_sample.reshape(dec_batch * dec_seq, D_MODEL),
        p_prompt.reshape(depth, batch * seq, D_PLE), p_sample.reshape(depth, dec_batch * dec_seq, D_PLE),
        jnp.zeros((depth, batch, HIST_A_PAD, W_BR), F32), _front_pad(cache_conv_a, HIST_A_PAD),
        jnp.zeros((depth, batch, HIST_B_PAD, W_BR), F32), _front_pad(cache_conv_b, HIST_B_PAD),
        w, raw, depth, seq, dec_seq)

    return (y_prompt.reshape(x_prompt.shape), y_sample.reshape(x_sample.shape),
            conv_a_prompt, conv_a_sample, conv_b_prompt, conv_b_sample,
            chunk_v.reshape(depth, dec_batch, dec_seq, W_BR))
```

```python
import functools
from typing import NamedTuple

import jax
import jax.numpy as jnp
from jax import lax
from jax.experimental import pallas as pl
from jax.experimental.pallas import tpu as pltpu

D_MODEL = 1024
D_FF = 2816
D_PLE = 256
W_BR = 512
CONV_A = 31
CONV_B = 3
H_C = 4
C_HEAD = W_BR // H_C
MLP_CHUNK = 128
EPS = 1e-6
COL_A, COL_B, COL_C, COL_G, COL_END = 0, 1024, 2560, 3584, 6656

HIST_A_PAD = 32
HIST_B_PAD = 8
SUBLANES = 8
BF16_ROWS = 16
LANES = 128
CONV_ACC_VREGS = 16
VMEM_LIMIT_BYTES = 56 * 1024 * 1024

BF16 = jnp.bfloat16
F32 = jnp.float32


def _dot(a, b):
    return jnp.dot(a, b, preferred_element_type=F32)


def _rmsnorm(x, g):
    return x * lax.rsqrt(jnp.mean(x * x, axis=-1, keepdims=True) + EPS) * g


def _layernorm(x, g, b):
    mu = jnp.mean(x, axis=-1, keepdims=True)
    xc = x - mu
    return xc * lax.rsqrt(jnp.mean(xc * xc, axis=-1, keepdims=True) + EPS) * g + b


def _swiglu_half_step(x, g, wg_ref, wu_ref, wd_ref):
    h = _rmsnorm(x, g).astype(BF16)
    act = jax.nn.silu(_dot(h, wg_ref[...])) * _dot(h, wu_ref[...])
    return x + 0.5 * _dot(act.astype(BF16), wd_ref[...])


def _two_groups(n_prompt_steps, prompt_body, sample_body):
    step = pl.program_id(0)
    pl.when(step < n_prompt_steps)(prompt_body)
    pl.when(step == n_prompt_steps)(sample_body)


def _ffn_kernel(n_prompt_steps, xp_ref, xs_ref, g_ref, wg_ref, wu_ref, wd_ref, op_ref, os_ref):
    def tile(x_ref, o_ref):
        o_ref[...] = _swiglu_half_step(x_ref[...], g_ref[...], wg_ref, wu_ref, wd_ref)

    _two_groups(n_prompt_steps, functools.partial(tile, xp_ref, op_ref), functools.partial(tile, xs_ref, os_ref))


def _ffn_ple_kernel(final, n_prompt_steps, xp_ref, xs_ref, pp_ref, ps_ref, g_ref, wg_ref, wu_ref, wd_ref,
                    pg_ref, wpg_ref, wpp_ref, fg_ref, op_ref, os_ref):
    def tile(x_ref, p_ref, o_ref):
        x = _swiglu_half_step(x_ref[...], g_ref[...], wg_ref, wu_ref, wd_ref)
        gate = jax.nn.sigmoid(_dot(_rmsnorm(x, pg_ref[...]).astype(BF16), wpg_ref[...]))
        x = x + gate * _dot(p_ref[...].astype(BF16), wpp_ref[...])
        if final:
            x = _rmsnorm(x, fg_ref[...])
        o_ref[...] = x

    _two_groups(n_prompt_steps, functools.partial(tile, xp_ref, pp_ref, op_ref),
                functools.partial(tile, xs_ref, ps_ref, os_ref))


def _causal_dwconv(zp_ref, zsh_ref, w_ref, n_taps, first_row, n_seq, n_rows):
    if zsh_ref is not None:
        span = zsh_ref.shape[2]
        for r in range(1, SUBLANES):
            zsh_ref[r - 1] = zp_ref[:, pl.ds(r, span), :]
    rows = min(n_rows, 32)
    seqs = max(1, min(n_seq, CONV_ACC_VREGS * SUBLANES * LANES // (rows * W_BR)))
    w = w_ref[...]
    seq_parts = []
    for s0 in range(0, n_seq, seqs):
        row_parts = []
        for r0 in range(0, n_rows, rows):
            acc = jnp.zeros((seqs, rows, W_BR), F32)
            for k in range(n_taps):
                shift, base = (first_row + k) % SUBLANES, (first_row + k) // SUBLANES * SUBLANES
                if zsh_ref is None:
                    win = zp_ref[pl.ds(s0, seqs), pl.ds(first_row + r0 + k, rows), :]
                elif shift == 0:
                    win = zp_ref[pl.ds(s0, seqs), pl.ds(base + r0, rows), :]
                else:
                    win = zsh_ref[shift - 1, pl.ds(s0, seqs), pl.ds(base + r0, rows), :]
                acc = acc + w[k] * win
            row_parts.append(acc)
        seq_parts.append(row_parts[0] if len(row_parts) == 1 else jnp.concatenate(row_parts, axis=1))
    return seq_parts[0] if len(seq_parts) == 1 else jnp.concatenate(seq_parts, axis=0)


def _spatial_mix(vn, wsp_ref, bsp_ref, n_seq, n_rows):
    t = min(n_rows, MLP_CHUNK)
    r_total = n_seq * n_rows
    tril = lax.broadcasted_iota(jnp.int32, (t, t), 0) >= lax.broadcasted_iota(jnp.int32, (t, t), 1)
    heads = []
    if t == MLP_CHUNK:
        n_chunks = r_total // t
        bias = bsp_ref[...]
        for h in range(H_C):
            cols = slice(h * C_HEAD, (h + 1) * C_HEAD)
            wm = jnp.where(tril, wsp_ref[h], 0.0).astype(BF16)
            rhs = jnp.concatenate([vn[j * t:(j + 1) * t, cols] for j in range(n_chunks)], axis=1)
            mh = _dot(wm, rhs) + bias[:, h:h + 1]
            heads.append(jnp.concatenate([mh[:, j * C_HEAD:(j + 1) * C_HEAD] for j in range(n_chunks)], axis=0))
    else:
        sel = (lax.broadcasted_iota(jnp.int32, (r_total, t), 0) % t
               == lax.broadcasted_iota(jnp.int32, (r_total, t), 1)).astype(F32)
        sel_t = (lax.broadcasted_iota(jnp.int32, (t, r_total), 1) % t
                 == lax.broadcasted_iota(jnp.int32, (t, r_total), 0)).astype(F32)
        same_seq = (lax.broadcasted_iota(jnp.int32, (r_total, r_total), 0) // t
                    == lax.broadcasted_iota(jnp.int32, (r_total, r_total), 1) // t)
        bias = jnp.concatenate([bsp_ref[0:t, :]] * n_seq, axis=0)
        for h in range(H_C):
            cols = slice(h * C_HEAD, (h + 1) * C_HEAD)
            wm = jnp.where(tril, wsp_ref[h, 0:t, 0:t], 0.0)
            big = jnp.where(same_seq, _dot(_dot(sel, wm), sel_t), 0.0).astype(BF16)
            heads.append(_dot(big, vn[:, cols]) + bias[:, h:h + 1])
    return jnp.concatenate(heads, axis=1)


def _mixer_tile(n_seq, n_rows, fresh, x_ref, hista_ref, histb_ref, params, o_ref, newa_ref, newb_ref,
                vn_ref, zpa_ref, zpb_ref, zsh_ref, h_ref, proj_ref):
    (g_ref, win_ref, gbias_ref, caw_ref, cab_ref, lnag_ref, lnab_ref, waout_ref, cbw_ref, wbout_ref,
     lncg_ref, lncb_ref, wsp_ref, bsp_ref, wcout_ref, wo_ref) = params
    r_total = n_seq * n_rows

    @pl.when(fresh)
    def _():
        zpa_ref[:, 0:HIST_A_PAD, :] = hista_ref[...]
        zpb_ref[:, 0:HIST_B_PAD, :] = histb_ref[...]

    h_ref[...] = _rmsnorm(x_ref[...], g_ref[...]).astype(BF16)

    proj_ref[:, COL_A:COL_B] = _dot(h_ref[...], win_ref[:, COL_A:COL_B])
    za = proj_ref[:, 0:W_BR] * jax.nn.sigmoid(proj_ref[:, W_BR:COL_B])
    zpa_ref[:, HIST_A_PAD:HIST_A_PAD + n_rows, :] = za.reshape(n_seq, n_rows, W_BR)

    proj_ref[:, COL_B:COL_G] = _dot(h_ref[...], win_ref[:, COL_B:COL_G])
    ca = _causal_dwconv(zpa_ref, zsh_ref, caw_ref, CONV_A, HIST_A_PAD - (CONV_A - 1), n_seq, n_rows)
    ca = ca.reshape(r_total, W_BR) + cab_ref[...]
    ya_in = jax.nn.silu(_layernorm(ca, lnag_ref[...], lnab_ref[...])).astype(BF16)
    tail_a = zpa_ref[:, n_rows:n_rows + HIST_A_PAD, :]
    zpa_ref[:, 0:HIST_A_PAD, :] = tail_a
    newa_ref[...] = tail_a

    proj_ref[:, COL_G:COL_END] = _dot(h_ref[...], win_ref[:, COL_G:COL_END])

    zb = proj_ref[:, COL_B + W_BR:COL_B + 2 * W_BR] * proj_ref[:, COL_B + 2 * W_BR:COL_C]
    zpb_ref[:, HIST_B_PAD:HIST_B_PAD + n_rows, :] = zb.reshape(n_seq, n_rows, W_BR)
    cb = _causal_dwconv(zpb_ref, None, cbw_ref, CONV_B, HIST_B_PAD - (CONV_B - 1), n_seq, n_rows)
    yb_in = (proj_ref[:, COL_B:COL_B + W_BR] * cb.reshape(r_total, W_BR)).astype(BF16)
    tail_b = zpb_ref[:, n_rows:n_rows + HIST_B_PAD, :]
    zpb_ref[:, 0:HIST_B_PAD, :] = tail_b
    newb_ref[...] = tail_b

    vn = _layernorm(jax.nn.gelu(proj_ref[:, COL_C + W_BR:COL_G]), lncg_ref[...], lncb_ref[...])
    if vn_ref is not None:
        vn_ref[...] = vn
    gu = jax.nn.gelu(proj_ref[:, COL_C:COL_C + W_BR])

    ya = _dot(ya_in, waout_ref[...])
    yb = _dot(yb_in, wbout_ref[...])
    mixed = _spatial_mix(vn.astype(BF16), wsp_ref, bsp_ref, n_seq, n_rows)
    yc = _dot((gu * mixed).astype(BF16), wcout_ref[...])

    gates = jax.nn.sigmoid(proj_ref[:, COL_G:COL_END] + gbias_ref[...])
    merged = (gates[:, :D_MODEL] * ya + gates[:, D_MODEL:2 * D_MODEL] * yb
              + gates[:, 2 * D_MODEL:] * yc)
    o_ref[...] = x_ref[...] + _dot(merged.astype(BF16), wo_ref[...])


N_MIXER_PARAMS = 16


def _mixer_kernel(n_seq, n_rows, tiles_per_seq, emit_v, x_ref, hista_ref, histb_ref, *refs):
    params = refs[:N_MIXER_PARAMS]
    o_ref, newa_ref, newb_ref = refs[N_MIXER_PARAMS:N_MIXER_PARAMS + 3]
    rest = refs[N_MIXER_PARAMS + 3:]
    vn_ref = rest[0] if emit_v else None
    _mixer_tile(n_seq, n_rows, pl.program_id(0) % tiles_per_seq == 0, x_ref, hista_ref, histb_ref, params,
                o_ref, newa_ref, newb_ref, vn_ref, *rest[1 if emit_v else 0:])


class _Cast(NamedTuple):
    src: jax.Array
    layer: int


def _cast_chunks(rows, n_steps):
    units = rows // BF16_ROWS
    n_chunks = max(d for d in range(1, min(units, n_steps) + 1) if units % d == 0)
    return rows // n_chunks, n_chunks


def _with_casts(body, n_in, n_out, n_casts, *refs):
    ins, cast_ins = refs[:n_in], refs[n_in:n_in + n_casts]
    outs = refs[n_in + n_casts:n_in + n_casts + n_out]
    cast_outs = refs[n_in + n_casts + n_out:n_in + 2 * n_casts + n_out]
    body(*ins, *outs, *refs[n_in + 2 * n_casts + n_out:])
    for src_ref, dst_ref in zip(cast_ins, cast_outs):
        dst_ref[...] = src_ref[...].astype(BF16)


def _fused_call(body, name, grid, operands, in_specs, out_shapes, out_specs, scratch, casts):
    n_steps = 1
    for g in grid:
        n_steps *= g
    linear = (lambda t: t) if len(grid) == 1 else (lambda b, t: b * grid[1] + t)
    c_in, c_out, c_shapes = [], [], []
    for c in casts:
        _, rows, cols = c.src.shape
        chunk, n_chunks = _cast_chunks(rows, n_steps)
        at = lambda *g, n=n_chunks: jnp.minimum(linear(*g), n - 1)
        c_in.append(pl.BlockSpec((None, chunk, cols), lambda *g, at=at, layer=c.layer: (layer, at(*g), 0)))
        c_out.append(pl.BlockSpec((chunk, cols), lambda *g, at=at: (at(*g), 0)))
        c_shapes.append(jax.ShapeDtypeStruct((rows, cols), BF16))
    outs = pl.pallas_call(
        functools.partial(_with_casts, body, len(operands), len(out_shapes), len(casts)),
        out_shape=tuple(out_shapes) + tuple(c_shapes),
        grid=grid,
        in_specs=list(in_specs) + c_in,
        out_specs=tuple(out_specs) + tuple(c_out),
        scratch_shapes=scratch,
        compiler_params=pltpu.CompilerParams(dimension_semantics=("arbitrary",) * len(grid),
                                             vmem_limit_bytes=VMEM_LIMIT_BYTES),
        name=name,
    )(*operands, *[c.src for c in casts])
    return outs[:len(out_shapes)], outs[len(out_shapes):]


def _layer_spec(layer, tail_shape):
    zeros = (0,) * len(tail_shape)
    return pl.BlockSpec((None,) + tuple(tail_shape), lambda *g: (layer,) + zeros,
                        pipeline_mode=pl.Buffered(1))


def _resident_spec(shape):
    return pl.BlockSpec(tuple(shape), lambda *g: (0, 0), pipeline_mode=pl.Buffered(1))


def _tile_spec(rows, width, n_prompt_steps):
    return pl.BlockSpec((rows, width), lambda t: (jnp.minimum(t, n_prompt_steps - 1), 0))


def _whole_spec(shape):
    return pl.BlockSpec(tuple(shape), lambda t: (0,) * len(shape))


def _ffn_call(xp, xs, layer, w, wb, tm, casts):
    n_p = xp.shape[0] // tm
    outs, cast_out = _fused_call(
        functools.partial(_ffn_kernel, n_p), "ffn1", (n_p + 1,),
        [xp, xs, w['ffn1_norm'], wb['ffn1_w_gate'], wb['ffn1_w_up'], wb['ffn1_w_down']],
        [_tile_spec(tm, D_MODEL, n_p), _whole_spec(xs.shape), _layer_spec(layer, (1, D_MODEL)),
         _resident_spec((D_MODEL, D_FF)), _resident_spec((D_MODEL, D_FF)), _resident_spec((D_FF, D_MODEL))],
        [jax.ShapeDtypeStruct(xp.shape, F32), jax.ShapeDtypeStruct(xs.shape, F32)],
        [_tile_spec(tm, D_MODEL, n_p), _whole_spec(xs.shape)], [], casts)
    return outs, cast_out


def _ffn_ple_call(xp, xs, pp, ps, layer, w, wb, tm, final, casts):
    n_p = xp.shape[0] // tm
    outs, cast_out = _fused_call(
        functools.partial(_ffn_ple_kernel, final, n_p), "ffn2_ple", (n_p + 1,),
        [xp, xs, pp, ps, w['ffn2_norm'], wb['ffn2_w_gate'], wb['ffn2_w_up'], wb['ffn2_w_down'],
         w['ple_norm'], wb['w_ple_gate'], wb['w_ple_proj'], w['final_norm']],
        [_tile_spec(tm, D_MODEL, n_p), _whole_spec(xs.shape),
         pl.BlockSpec((None, tm, D_PLE), lambda t: (layer, jnp.minimum(t, n_p - 1), 0)),
         pl.BlockSpec((None,) + ps.shape[1:], lambda t: (layer, 0, 0)),
         _layer_spec(layer, (1, D_MODEL)), _resident_spec((D_MODEL, D_FF)),
         _resident_spec((D_MODEL, D_FF)), _resident_spec((D_FF, D_MODEL)),
         _layer_spec(layer, (1, D_MODEL)), _resident_spec((D_MODEL, D_MODEL)),
         _resident_spec((D_PLE, D_MODEL)), _resident_spec((1, D_MODEL))],
        [jax.ShapeDtypeStruct(xp.shape, F32), jax.ShapeDtypeStruct(xs.shape, F32)],
        [_tile_spec(tm, D_MODEL, n_p), _whole_spec(xs.shape)], [], casts)
    return outs, cast_out


def _mixer_call(x, hist_a, hist_b, layer, w, wb, seq, tile_rows, emit_v, casts):
    rows = min(seq, tile_rows)
    seqs = tile_rows // rows
    tiles_per_seq = seq // rows
    n_steps = x.shape[0] // tile_rows
    ls = functools.partial(_layer_spec, layer)
    x_spec = lambda width: pl.BlockSpec((tile_rows, width), lambda t: (t, 0))
    state = lambda pad: pl.BlockSpec((seqs, pad, W_BR), lambda t: (t // tiles_per_seq, 0, 0))
    in_specs = [
        x_spec(D_MODEL), state(HIST_A_PAD), state(HIST_B_PAD),
        ls((1, D_MODEL)), _resident_spec((D_MODEL, COL_END)), ls((1, COL_END - COL_G)),
        ls((CONV_A, W_BR)), ls((1, W_BR)), ls((1, W_BR)), ls((1, W_BR)), _resident_spec((W_BR, D_MODEL)),
        ls((CONV_B, W_BR)), _resident_spec((W_BR, D_MODEL)),
        ls((1, W_BR)), ls((1, W_BR)), ls((H_C, MLP_CHUNK, MLP_CHUNK)), ls((MLP_CHUNK, H_C)),
        _resident_spec((W_BR, D_MODEL)), _resident_spec((D_MODEL, D_MODEL)),
    ]
    f32 = lambda *shape: jax.ShapeDtypeStruct(shape, F32)
    out_shapes = [f32(*x.shape), f32(*hist_a.shape), f32(*hist_b.shape)]
    out_specs = [x_spec(D_MODEL), state(HIST_A_PAD), state(HIST_B_PAD)]
    if emit_v:
        out_shapes.append(f32(x.shape[0], W_BR))
        out_specs.append(x_spec(W_BR))
    scratch = [pltpu.VMEM((seqs, HIST_A_PAD + rows, W_BR), F32),
               pltpu.VMEM((seqs, HIST_B_PAD + rows, W_BR), F32),
               pltpu.VMEM((SUBLANES - 1, seqs, HIST_A_PAD - SUBLANES + rows, W_BR), F32),
               pltpu.VMEM((tile_rows, D_MODEL), BF16), pltpu.VMEM((tile_rows, COL_END), F32)]
    params = [w['mix_norm'], wb['w_in'], w['gate_bias'],
              w['conv_a_w'], w['conv_a_b'], w['ln_a_g'], w['ln_a_b'], wb['w_a_out'],
              w['conv_b_w'], wb['w_b_out'], w['ln_c_g'], w['ln_c_b'], w['w_spatial'], w['b_spatial'],
              wb['w_c_out'], wb['w_o']]
    assert len(params) == N_MIXER_PARAMS
    outs, cast_out = _fused_call(
        functools.partial(_mixer_kernel, seqs, rows, tiles_per_seq, emit_v), "mixer",
        (n_steps,), [x, hist_a, hist_b] + params, in_specs, out_shapes, out_specs, scratch, casts)
    return (*outs, None)[:4], cast_out


FFN1_WEIGHTS = ('ffn1_w_gate', 'ffn1_w_up', 'ffn1_w_down')
MIXER_WEIGHTS = ('w_in', 'w_a_out', 'w_b_out', 'w_c_out', 'w_o')
FFN2_WEIGHTS = ('ffn2_w_gate', 'ffn2_w_up', 'ffn2_w_down', 'w_ple_gate', 'w_ple_proj')
CAST_ONLY_STEPS = 16


MIXER_TILE_ROWS = 256
FFN_TILE_ROWS = 512


def _trunk(xp, xs, pp, ps, hist_ap, hist_as, hist_bp, hist_bs, w, raw, depth, seq, dec_seq):
    casts = lambda names, layer: [_Cast(raw[n], layer) for n in names] if layer < depth else []
    wb = [dict() for _ in range(depth)]
    _, done = _fused_call(lambda: None, "cast", (CAST_ONLY_STEPS,), [], [], [], [], [], casts(FFN1_WEIGHTS, 0))
    wb[0].update(zip(FFN1_WEIGHTS, done))
    states, chunk_v = [], []
    for i in range(depth):
        (xp, xs), done = _ffn_call(xp, xs, i, w, wb[i], FFN_TILE_ROWS, casts(MIXER_WEIGHTS, i))
        wb[i].update(zip(MIXER_WEIGHTS, done))
        (xp, nap, nbp, _), done = _mixer_call(xp, hist_ap[i], hist_bp[i], i, w, wb[i], seq, MIXER_TILE_ROWS,
                                              False, casts(FFN2_WEIGHTS, i))
        wb[i].update(zip(FFN2_WEIGHTS, done))
        (xs, nas, nbs, vn), _ = _mixer_call(xs, hist_as[i], hist_bs[i], i, w, wb[i], dec_seq, MIXER_TILE_ROWS,
                                            True, [])
        (xp, xs), done = _ffn_ple_call(xp, xs, pp, ps, i, w, wb[i], FFN_TILE_ROWS, i == depth - 1,
                                       casts(FFN1_WEIGHTS, i + 1))
        if done:
            wb[i + 1].update(zip(FFN1_WEIGHTS, done))
        states.append([nap[:, HIST_A_PAD - (CONV_A - 1):], nas[:, HIST_A_PAD - (CONV_A - 1):],
                       nbp[:, HIST_B_PAD - (CONV_B - 1):], nbs[:, HIST_B_PAD - (CONV_B - 1):]])
        chunk_v.append(vn)
    return (xp, xs, *[jnp.stack(s) for s in zip(*states)], jnp.stack(chunk_v))


def _front_pad(hist, pad_to):
    return jnp.pad(hist, ((0, 0), (0, 0), (pad_to - hist.shape[2], 0), (0, 0)))


def kernel(x_prompt, x_sample, p_prompt, p_sample, cache_conv_a, cache_conv_b, ffn1_norm, ffn1_w_gate, ffn1_w_up, ffn1_w_down, mix_norm, w_in, gate_bias, conv_a_w, conv_a_b, ln_a_g, ln_a_b, w_a_out, conv_b_w, w_b_out, ln_c_g, ln_c_b, w_spatial, b_spatial, w_c_out, w_o, ffn2_norm, ffn2_w_gate, ffn2_w_up, ffn2_w_down, ple_norm, w_ple_gate, w_ple_proj, final_norm):
    depth = w_in.shape[0]
    batch, seq, _ = x_prompt.shape
    dec_batch, dec_seq, _ = x_sample.shape
    row = lambda a: a.reshape(a.shape[0], 1, a.shape[1])
    w = dict(
        ffn1_norm=row(ffn1_norm), mix_norm=row(mix_norm), gate_bias=row(gate_bias),
        conv_a_w=conv_a_w, conv_a_b=row(conv_a_b), ln_a_g=row(ln_a_g), ln_a_b=row(ln_a_b),
        conv_b_w=conv_b_w, ln_c_g=row(ln_c_g), ln_c_b=row(ln_c_b), w_spatial=w_spatial,
        b_spatial=jnp.swapaxes(b_spatial, 1, 2), ffn2_norm=row(ffn2_norm), ple_norm=row(ple_norm),
        final_norm=final_norm.reshape(1, D_MODEL))
    raw = dict(ffn1_w_gate=ffn1_w_gate, ffn1_w_up=ffn1_w_up, ffn1_w_down=ffn1_w_down,
               w_in=w_in, w_a_out=w_a_out, w_b_out=w_b_out, w_c_out=w_c_out, w_o=w_o,
               ffn2_w_gate=ffn2_w_gate, ffn2_w_up=ffn2_w_up, ffn2_w_down=ffn2_w_down,
               w_ple_gate=w_ple_gate, w_ple_proj=w_ple_proj)

    (y_prompt, y_sample, conv_a_prompt, conv_a_sample, conv_b_prompt, conv_b_sample, chunk_v) = _trunk(
        x_prompt.reshape(batch * seq, D_MODEL), x_sample.reshape(dec_batch * dec_seq, D_MODEL),
        p_prompt.reshape(depth, batch * seq, D_PLE), p_sample.reshape(depth, dec_batch * dec_seq, D_PLE),
        jnp.zeros((depth, batch, HIST_A_PAD, W_BR), F32), _front_pad(cache_conv_a, HIST_A_PAD),
        jnp.zeros((depth, batch, HIST_B_PAD, W_BR), F32), _front_pad(cache_conv_b, HIST_B_PAD),
        w, raw, depth, seq, dec_seq)

    return (y_prompt.reshape(x_prompt.shape), y_sample.reshape(x_sample.shape),
            conv_a_prompt, conv_a_sample, conv_b_prompt, conv_b_sample,
            chunk_v.reshape(depth, dec_batch, dec_seq, W_BR))
```

```python
import functools
from typing import NamedTuple

import jax
import jax.numpy as jnp
from jax import lax
from jax.experimental import pallas as pl
from jax.experimental.pallas import tpu as pltpu

D_MODEL = 1024
D_FF = 2816
D_PLE = 256
W_BR = 512
CONV_A = 31
CONV_B = 3
H_C = 4
C_HEAD = W_BR // H_C
MLP_CHUNK = 128
EPS = 1e-6
COL_A, COL_B, COL_C, COL_G, COL_END = 0, 1024, 2560, 3584, 6656

HIST_A_PAD = 32
HIST_B_PAD = 8
SUBLANES = 8
BF16_ROWS = 16
LANES = 128
CONV_ACC_VREGS = 16
VMEM_LIMIT_BYTES = 56 * 1024 * 1024

BF16 = jnp.bfloat16
F32 = jnp.float32


def _dot(a, b):
    return jnp.dot(a, b, preferred_element_type=F32)


def _rmsnorm(x, g):
    return x * lax.rsqrt(jnp.mean(x * x, axis=-1, keepdims=True) + EPS) * g


def _layernorm(x, g, b):
    mu = jnp.mean(x, axis=-1, keepdims=True)
    xc = x - mu
    return xc * lax.rsqrt(jnp.mean(xc * xc, axis=-1, keepdims=True) + EPS) * g + b


def _swiglu_half_step(x, g, wg_ref, wu_ref, wd_ref):
    h = _rmsnorm(x, g).astype(BF16)
    act = jax.nn.silu(_dot(h, wg_ref[...])) * _dot(h, wu_ref[...])
    return x + 0.5 * _dot(act.astype(BF16), wd_ref[...])


def _two_groups(n_prompt_steps, prompt_body, sample_body):
    step = pl.program_id(0)
    pl.when(step < n_prompt_steps)(prompt_body)
    pl.when(step == n_prompt_steps)(sample_body)


def _ffn_kernel(n_prompt_steps, xp_ref, xs_ref, g_ref, wg_ref, wu_ref, wd_ref, op_ref, os_ref):
    def tile(x_ref, o_ref):
        o_ref[...] = _swiglu_half_step(x_ref[...], g_ref[...], wg_ref, wu_ref, wd_ref)

    _two_groups(n_prompt_steps, functools.partial(tile, xp_ref, op_ref), functools.partial(tile, xs_ref, os_ref))


def _ffn_ple_kernel(final, n_prompt_steps, xp_ref, xs_ref, pp_ref, ps_ref, g_ref, wg_ref, wu_ref, wd_ref,
                    pg_ref, wpg_ref, wpp_ref, fg_ref, op_ref, os_ref):
    def tile(x_ref, p_ref, o_ref):
        x = _swiglu_half_step(x_ref[...], g_ref[...], wg_ref, wu_ref, wd_ref)
        gate = jax.nn.sigmoid(_dot(_rmsnorm(x, pg_ref[...]).astype(BF16), wpg_ref[...]))
        x = x + gate * _dot(p_ref[...].astype(BF16), wpp_ref[...])
        if final:
            x = _rmsnorm(x, fg_ref[...])
        o_ref[...] = x

    _two_groups(n_prompt_steps, functools.partial(tile, xp_ref, pp_ref, op_ref),
                functools.partial(tile, xs_ref, ps_ref, os_ref))


def _causal_dwconv(zp_ref, zsh_ref, w_ref, n_taps, first_row, n_seq, n_rows):
    if zsh_ref is not None:
        span = zsh_ref.shape[2]
        for r in range(1, SUBLANES):
            zsh_ref[r - 1] = zp_ref[:, pl.ds(r, span), :]
    rows = min(n_rows, 32)
    seqs = max(1, min(n_seq, CONV_ACC_VREGS * SUBLANES * LANES // (rows * W_BR)))
    w = w_ref[...]
    seq_parts = []
    for s0 in range(0, n_seq, seqs):
        row_parts = []
        for r0 in range(0, n_rows, rows):
            acc = jnp.zeros((seqs, rows, W_BR), F32)
            for k in range(n_taps):
                shift, base = (first_row + k) % SUBLANES, (first_row + k) // SUBLANES * SUBLANES
                if zsh_ref is None:
                    win = zp_ref[pl.ds(s0, seqs), pl.ds(first_row + r0 + k, rows), :]
                elif shift == 0:
                    win = zp_ref[pl.ds(s0, seqs), pl.ds(base + r0, rows), :]
                else:
                    win = zsh_ref[shift - 1, pl.ds(s0, seqs), pl.ds(base + r0, rows), :]
                acc = acc + w[k] * win
            row_parts.append(acc)
        seq_parts.append(row_parts[0] if len(row_parts) == 1 else jnp.concatenate(row_parts, axis=1))
    return seq_parts[0] if len(seq_parts) == 1 else jnp.concatenate(seq_parts, axis=0)


def _spatial_mix(vn, wsp_ref, bsp_ref, n_seq, n_rows):
    t = min(n_rows, MLP_CHUNK)
    r_total = n_seq * n_rows
    tril = lax.broadcasted_iota(jnp.int32, (t, t), 0) >= lax.broadcasted_iota(jnp.int32, (t, t), 1)
    heads = []
    if t == MLP_CHUNK:
        n_chunks = r_total // t
        bias = bsp_ref[...]
        for h in range(H_C):
            cols = slice(h * C_HEAD, (h + 1) * C_HEAD)
            wm = jnp.where(tril, wsp_ref[h], 0.0).astype(BF16)
            rhs = jnp.concatenate([vn[j * t:(j + 1) * t, cols] for j in range(n_chunks)], axis=1)
            mh = _dot(wm, rhs) + bias[:, h:h + 1]
            heads.append(jnp.concatenate([mh[:, j * C_HEAD:(j + 1) * C_HEAD] for j in range(n_chunks)], axis=0))
    else:
        sel = (lax.broadcasted_iota(jnp.int32, (r_total, t), 0) % t
               == lax.broadcasted_iota(jnp.int32, (r_total, t), 1)).astype(F32)
        sel_t = (lax.broadcasted_iota(jnp.int32, (t, r_total), 1) % t
                 == lax.broadcasted_iota(jnp.int32, (t, r_total), 0)).astype(F32)
        same_seq = (lax.broadcasted_iota(jnp.int32, (r_total, r_total), 0) // t
                    == lax.broadcasted_iota(jnp.int32, (r_total, r_total), 1) // t)
        bias = jnp.concatenate([bsp_ref[0:t, :]] * n_seq, axis=0)
        for h in range(H_C):
            cols = slice(h * C_HEAD, (h + 1) * C_HEAD)
            wm = jnp.where(tril, wsp_ref[h, 0:t, 0:t], 0.0)
            big = jnp.where(same_seq, _dot(_dot(sel, wm), sel_t), 0.0).astype(BF16)
            heads.append(_dot(big, vn[:, cols]) + bias[:, h:h + 1])
    return jnp.concatenate(heads, axis=1)


def _mixer_tile(n_seq, n_rows, fresh, x_ref, hista_ref, histb_ref, params, o_ref, newa_ref, newb_ref,
                vn_ref, zpa_ref, zpb_ref, zsh_ref, h_ref, proj_ref):
    (g_ref, win_ref, gbias_ref, caw_ref, cab_ref, lnag_ref, lnab_ref, waout_ref, cbw_ref, wbout_ref,
     lncg_ref, lncb_ref, wsp_ref, bsp_ref, wcout_ref, wo_ref) = params
    r_total = n_seq * n_rows

    @pl.when(fresh)
    def _():
        zpa_ref[:, 0:HIST_A_PAD, :] = hista_ref[...]
        zpb_ref[:, 0:HIST_B_PAD, :] = histb_ref[...]

    h_ref[...] = _rmsnorm(x_ref[...], g_ref[...]).astype(BF16)

    proj_ref[:, COL_A:COL_B] = _dot(h_ref[...], win_ref[:, COL_A:COL_B])
    za = proj_ref[:, 0:W_BR] * jax.nn.sigmoid(proj_ref[:, W_BR:COL_B])
    zpa_ref[:, HIST_A_PAD:HIST_A_PAD + n_rows, :] = za.reshape(n_seq, n_rows, W_BR)

    proj_ref[:, COL_B:COL_G] = _dot(h_ref[...], win_ref[:, COL_B:COL_G])
    ca = _causal_dwconv(zpa_ref, zsh_ref, caw_ref, CONV_A, HIST_A_PAD - (CONV_A - 1), n_seq, n_rows)
    ca = ca.reshape(r_total, W_BR) + cab_ref[...]
    ya_in = jax.nn.silu(_layernorm(ca, lnag_ref[...], lnab_ref[...])).astype(BF16)
    tail_a = zpa_ref[:, n_rows:n_rows + HIST_A_PAD, :]
    zpa_ref[:, 0:HIST_A_PAD, :] = tail_a
    newa_ref[...] = tail_a

    proj_ref[:, COL_G:COL_END] = _dot(h_ref[...], win_ref[:, COL_G:COL_END])

    zb = proj_ref[:, COL_B + W_BR:COL_B + 2 * W_BR] * proj_ref[:, COL_B + 2 * W_BR:COL_C]
    zpb_ref[:, HIST_B_PAD:HIST_B_PAD + n_rows, :] = zb.reshape(n_seq, n_rows, W_BR)
    cb = _causal_dwconv(zpb_ref, None, cbw_ref, CONV_B, HIST_B_PAD - (CONV_B - 1), n_seq, n_rows)
    yb_in = (proj_ref[:, COL_B:COL_B + W_BR] * cb.reshape(r_total, W_BR)).astype(BF16)
    tail_b = zpb_ref[:, n_rows:n_rows + HIST_B_PAD, :]
    zpb_ref[:, 0:HIST_B_PAD, :] = tail_b
    newb_ref[...] = tail_b

    vn = _layernorm(jax.nn.gelu(proj_ref[:, COL_C + W_BR:COL_G]), lncg_ref[...], lncb_ref[...])
    if vn_ref is not None:
        vn_ref[...] = vn
    gu = jax.nn.gelu(proj_ref[:, COL_C:COL_C + W_BR])

    ya = _dot(ya_in, waout_ref[...])
    yb = _dot(yb_in, wbout_ref[...])
    mixed = _spatial_mix(vn.astype(BF16), wsp_ref, bsp_ref, n_seq, n_rows)
    yc = _dot((gu * mixed).astype(BF16), wcout_ref[...])

    gates = jax.nn.sigmoid(proj_ref[:, COL_G:COL_END] + gbias_ref[...])
    merged = (gates[:, :D_MODEL] * ya + gates[:, D_MODEL:2 * D_MODEL] * yb
              + gates[:, 2 * D_MODEL:] * yc)
    o_ref[...] = x_ref[...] + _dot(merged.astype(BF16), wo_ref[...])


N_MIXER_PARAMS = 16


def _mixer_kernel(n_seq, n_rows, tiles_per_seq, emit_v, x_ref, hista_ref, histb_ref, *refs):
    params = refs[:N_MIXER_PARAMS]
    o_ref, newa_ref, newb_ref = refs[N_MIXER_PARAMS:N_MIXER_PARAMS + 3]
    rest = refs[N_MIXER_PARAMS + 3:]
    vn_ref = rest[0] if emit_v else None
    _mixer_tile(n_seq, n_rows, pl.program_id(0) % tiles_per_seq == 0, x_ref, hista_ref, histb_ref, params,
                o_ref, newa_ref, newb_ref, vn_ref, *rest[1 if emit_v else 0:])


class _Cast(NamedTuple):
    src: jax.Array
    layer: int


def _cast_chunks(rows, n_steps):
    units = rows // BF16_ROWS
    n_chunks = max(d for d in range(1, min(units, n_steps) + 1) if units % d == 0)
    return rows // n_chunks, n_chunks


def _with_casts(body, n_in, n_out, n_casts, *refs):
    ins, cast_ins = refs[:n_in], refs[n_in:n_in + n_casts]
    outs = refs[n_in + n_casts:n_in + n_casts + n_out]
    cast_outs = refs[n_in + n_casts + n_out:n_in + 2 * n_casts + n_out]
    body(*ins, *outs, *refs[n_in + 2 * n_casts + n_out:])
    for src_ref, dst_ref in zip(cast_ins, cast_outs):
        dst_ref[...] = src_ref[...].astype(BF16)


def _fused_call(body, name, grid, operands, in_specs, out_shapes, out_specs, scratch, casts):
    n_steps = 1
    for g in grid:
        n_steps *= g
    linear = (lambda t: t) if len(grid) == 1 else (lambda b, t: b * grid[1] + t)
    c_in, c_out, c_shapes = [], [], []
    for c in casts:
        _, rows, cols = c.src.shape
        chunk, n_chunks = _cast_chunks(rows, n_steps)
        at = lambda *g, n=n_chunks: jnp.minimum(linear(*g), n - 1)
        c_in.append(pl.BlockSpec((None, chunk, cols), lambda *g, at=at, layer=c.layer: (layer, at(*g), 0)))
        c_out.append(pl.BlockSpec((chunk, cols), lambda *g, at=at: (at(*g), 0)))
        c_shapes.append(jax.ShapeDtypeStruct((rows, cols), BF16))
    outs = pl.pallas_call(
        functools.partial(_with_casts, body, len(operands), len(out_shapes), len(casts)),
        out_shape=tuple(out_shapes) + tuple(c_shapes),
        grid=grid,
        in_specs=list(in_specs) + c_in,
        out_specs=tuple(out_specs) + tuple(c_out),
        scratch_shapes=scratch,
        compiler_params=pltpu.CompilerParams(dimension_semantics=("arbitrary",) * len(grid),
                                             vmem_limit_bytes=VMEM_LIMIT_BYTES),
        name=name,
    )(*operands, *[c.src for c in casts])
    return outs[:len(out_shapes)], outs[len(out_shapes):]


def _layer_spec(layer, tail_shape):
    zeros = (0,) * len(tail_shape)
    return pl.BlockSpec((None,) + tuple(tail_shape), lambda *g: (layer,) + zeros,
                        pipeline_mode=pl.Buffered(1))


def _resident_spec(shape):
    return pl.BlockSpec(tuple(shape), lambda *g: (0, 0), pipeline_mode=pl.Buffered(1))


def _tile_spec(rows, width, n_prompt_steps):
    return pl.BlockSpec((rows, width), lambda t: (jnp.minimum(t, n_prompt_steps - 1), 0))


def _whole_spec(shape):
    return pl.BlockSpec(tuple(shape), lambda t: (0,) * len(shape))


def _ffn_call(xp, xs, layer, w, wb, tm, casts):
    n_p = xp.shape[0] // tm
    outs, cast_out = _fused_call(
        functools.partial(_ffn_kernel, n_p), "ffn1", (n_p + 1,),
        [xp, xs, w['ffn1_norm'], wb['ffn1_w_gate'], wb['ffn1_w_up'], wb['ffn1_w_down']],
        [_tile_spec(tm, D_MODEL, n_p), _whole_spec(xs.shape), _layer_spec(layer, (1, D_MODEL)),
         _resident_spec((D_MODEL, D_FF)), _resident_spec((D_MODEL, D_FF)), _resident_spec((D_FF, D_MODEL))],
        [jax.ShapeDtypeStruct(xp.shape, F32), jax.ShapeDtypeStruct(xs.shape, F32)],
        [_tile_spec(tm, D_MODEL, n_p), _whole_spec(xs.shape)], [], casts)
    return outs, cast_out


def _ffn_ple_call(xp, xs, pp, ps, layer, w, wb, tm, final, casts):
    n_p = xp.shape[0] // tm
    outs, cast_out = _fused_call(
        functools.partial(_ffn_ple_kernel, final, n_p), "ffn2_ple", (n_p + 1,),
        [xp, xs, pp, ps, w['ffn2_norm'], wb['ffn2_w_gate'], wb['ffn2_w_up'], wb['ffn2_w_down'],
         w['ple_norm'], wb['w_ple_gate'], wb['w_ple_proj'], w['final_norm']],
        [_tile_spec(tm, D_MODEL, n_p), _whole_spec(xs.shape),
         pl.BlockSpec((None, tm, D_PLE), lambda t: (layer, jnp.minimum(t, n_p - 1), 0)),
         pl.BlockSpec((None,) + ps.shape[1:], lambda t: (layer, 0, 0)),
         _layer_spec(layer, (1, D_MODEL)), _resident_spec((D_MODEL, D_FF)),
         _resident_spec((D_MODEL, D_FF)), _resident_spec((D_FF, D_MODEL)),
         _layer_spec(layer, (1, D_MODEL)), _resident_spec((D_MODEL, D_MODEL)),
         _resident_spec((D_PLE, D_MODEL)), _resident_spec((1, D_MODEL))],
        [jax.ShapeDtypeStruct(xp.shape, F32), jax.ShapeDtypeStruct(xs.shape, F32)],
        [_tile_spec(tm, D_MODEL, n_p), _whole_spec(xs.shape)], [], casts)
    return outs, cast_out


def _mixer_call(x, hist_a, hist_b, layer, w, wb, seq, tile_rows, emit_v, casts):
    rows = min(seq, tile_rows)
    seqs = tile_rows // rows
    tiles_per_seq = seq // rows
    n_steps = x.shape[0] // tile_rows
    ls = functools.partial(_layer_spec, layer)
    x_spec = lambda width: pl.BlockSpec((tile_rows, width), lambda t: (t, 0))
    state = lambda pad: pl.BlockSpec((seqs, pad, W_BR), lambda t: (t // tiles_per_seq, 0, 0))
    in_specs = [
        x_spec(D_MODEL), state(HIST_A_PAD), state(HIST_B_PAD),
        ls((1, D_MODEL)), _resident_spec((D_MODEL, COL_END)), ls((1, COL_END - COL_G)),
        ls((CONV_A, W_BR)), ls((1, W_BR)), ls((1, W_BR)), ls((1, W_BR)), _resident_spec((W_BR, D_MODEL)),
        ls((CONV_B, W_BR)), _resident_spec((W_BR, D_MODEL)),
        ls((1, W_BR)), ls((1, W_BR)), ls((H_C, MLP_CHUNK, MLP_CHUNK)), ls((MLP_CHUNK, H_C)),
        _resident_spec((W_BR, D_MODEL)), _resident_spec((D_MODEL, D_MODEL)),
    ]
    f32 = lambda *shape: jax.ShapeDtypeStruct(shape, F32)
    out_shapes = [f32(*x.shape), f32(*hist_a.shape), f32(*hist_b.shape)]
    out_specs = [x_spec(D_MODEL), state(HIST_A_PAD), state(HIST_B_PAD)]
    if emit_v:
        out_shapes.append(f32(x.shape[0], W_BR))
        out_specs.append(x_spec(W_BR))
    scratch = [pltpu.VMEM((seqs, HIST_A_PAD + rows, W_BR), F32),
               pltpu.VMEM((seqs, HIST_B_PAD + rows, W_BR), F32),
               pltpu.VMEM((SUBLANES - 1, seqs, HIST_A_PAD - SUBLANES + rows, W_BR), F32),
               pltpu.VMEM((tile_rows, D_MODEL), BF16), pltpu.VMEM((tile_rows, COL_END), F32)]
    params = [w['mix_norm'], wb['w_in'], w['gate_bias'],
              w['conv_a_w'], w['conv_a_b'], w['ln_a_g'], w['ln_a_b'], wb['w_a_out'],
              w['conv_b_w'], wb['w_b_out'], w['ln_c_g'], w['ln_c_b'], w['w_spatial'], w['b_spatial'],
              wb['w_c_out'], wb['w_o']]
    assert len(params) == N_MIXER_PARAMS
    outs, cast_out = _fused_call(
        functools.partial(_mixer_kernel, seqs, rows, tiles_per_seq, emit_v), "mixer",
        (n_steps,), [x, hist_a, hist_b] + params, in_specs, out_shapes, out_specs, scratch, casts)
    return (*outs, None)[:4], cast_out


FFN1_WEIGHTS = ('ffn1_w_gate', 'ffn1_w_up', 'ffn1_w_down')
MIXER_WEIGHTS = ('w_in', 'w_a_out', 'w_b_out', 'w_c_out', 'w_o')
FFN2_WEIGHTS = ('ffn2_w_gate', 'ffn2_w_up', 'ffn2_w_down', 'w_ple_gate', 'w_ple_proj')
CAST_ONLY_STEPS = 16


MIXER_TILE_ROWS = 256
FFN_TILE_ROWS = 512


def _trunk(xp, xs, pp, ps, hist_ap, hist_as, hist_bp, hist_bs, w, raw, depth, seq, dec_seq):
    casts = lambda names, layer: [_Cast(raw[n], layer) for n in names] if layer < depth else []
    wb = [dict() for _ in range(depth)]
    _, done = _fused_call(lambda: None, "cast", (CAST_ONLY_STEPS,), [], [], [], [], [], casts(FFN1_WEIGHTS, 0))
    wb[0].update(zip(FFN1_WEIGHTS, done))
    states, chunk_v = [], []
    for i in range(depth):
        (xp, xs), done = _ffn_call(xp, xs, i, w, wb[i], FFN_TILE_ROWS, casts(MIXER_WEIGHTS + FFN2_WEIGHTS, i))
        wb[i].update(zip(MIXER_WEIGHTS + FFN2_WEIGHTS, done))
        (xp, nap, nbp, _), _ = _mixer_call(xp, hist_ap[i], hist_bp[i], i, w, wb[i], seq, MIXER_TILE_ROWS,
                                           False, [])
        (xs, nas, nbs, vn), _ = _mixer_call(xs, hist_as[i], hist_bs[i], i, w, wb[i], dec_seq, MIXER_TILE_ROWS,
                                            True, [])
        (xp, xs), done = _ffn_ple_call(xp, xs, pp, ps, i, w, wb[i], FFN_TILE_ROWS, i == depth - 1,
                                       casts(FFN1_WEIGHTS, i + 1))
        if done:
            wb[i + 1].update(zip(FFN1_WEIGHTS, done))
        states.append([nap[:, HIST_A_PAD - (CONV_A - 1):], nas[:, HIST_A_PAD - (CONV_A - 1):],
                       nbp[:, HIST_B_PAD - (CONV_B - 1):], nbs[:, HIST_B_PAD - (CONV_B - 1):]])
        chunk_v.append(vn)
    return (xp, xs, *[jnp.stack(s) for s in zip(*states)], jnp.stack(chunk_v))


def _front_pad(hist, pad_to):
    return jnp.pad(hist, ((0, 0), (0, 0), (pad_to - hist.shape[2], 0), (0, 0)))


def kernel(x_prompt, x_sample, p_prompt, p_sample, cache_conv_a, cache_conv_b, ffn1_norm, ffn1_w_gate, ffn1_w_up, ffn1_w_down, mix_norm, w_in, gate_bias, conv_a_w, conv_a_b, ln_a_g, ln_a_b, w_a_out, conv_b_w, w_b_out, ln_c_g, ln_c_b, w_spatial, b_spatial, w_c_out, w_o, ffn2_norm, ffn2_w_gate, ffn2_w_up, ffn2_w_down, ple_norm, w_ple_gate, w_ple_proj, final_norm):
    depth = w_in.shape[0]
    batch, seq, _ = x_prompt.shape
    dec_batch, dec_seq, _ = x_sample.shape
    row = lambda a: a.reshape(a.shape[0], 1, a.shape[1])
    w = dict(
        ffn1_norm=row(ffn1_norm), mix_norm=row(mix_norm), gate_bias=row(gate_bias),
        conv_a_w=conv_a_w, conv_a_b=row(conv_a_b), ln_a_g=row(ln_a_g), ln_a_b=row(ln_a_b),
        conv_b_w=conv_b_w, ln_c_g=row(ln_c_g), ln_c_b=row(ln_c_b), w_spatial=w_spatial,
        b_spatial=jnp.swapaxes(b_spatial, 1, 2), ffn2_norm=row(ffn2_norm), ple_norm=row(ple_norm),
        final_norm=final_norm.reshape(1, D_MODEL))
    raw = dict(ffn1_w_gate=ffn1_w_gate, ffn1_w_up=ffn1_w_up, ffn1_w_down=ffn1_w_down,
               w_in=w_in, w_a_out=w_a_out, w_b_out=w_b_out, w_c_out=w_c_out, w_o=w_o,
               ffn2_w_gate=ffn2_w_gate, ffn2_w_up=ffn2_w_up, ffn2_w_down=ffn2_w_down,
               w_ple_gate=w_ple_gate, w_ple_proj=w_ple_proj)

    (y_prompt, y_sample, conv_a_prompt, conv_a_sample, conv_b_prompt, conv_b_sample, chunk_v) = _trunk(
        x_prompt.reshape(batch * seq, D_MODEL), x_sample.reshape(dec_batch * dec_seq, D_MODEL),
        p_prompt.reshape(depth, batch * seq, D_PLE), p_sample.reshape(depth, dec_batch * dec_seq, D_PLE),
        jnp.zeros((depth, batch, HIST_A_PAD, W_BR), F32), _front_pad(cache_conv_a, HIST_A_PAD),
        jnp.zeros((depth, batch, HIST_B_PAD, W_BR), F32), _front_pad(cache_conv_b, HIST_B_PAD),
        w, raw, depth, seq, dec_seq)

    return (y_prompt.reshape(x_prompt.shape), y_sample.reshape(x_sample.shape),
            conv_a_prompt, conv_a_sample, conv_b_prompt, conv_b_sample,
            chunk_v.reshape(depth, dec_batch, dec_seq, W_BR))
```

```python
import functools
from typing import NamedTuple

import jax
import jax.numpy as jnp
from jax import lax
from jax.experimental import pallas as pl
from jax.experimental.pallas import tpu as pltpu

D_MODEL = 1024
D_FF = 2816
D_PLE = 256
W_BR = 512
CONV_A = 31
CONV_B = 3
H_C = 4
C_HEAD = W_BR // H_C
MLP_CHUNK = 128
EPS = 1e-6
COL_A, COL_B, COL_C, COL_G, COL_END = 0, 1024, 2560, 3584, 6656

HIST_A_PAD = 32
HIST_B_PAD = 8
SUBLANES = 8
BF16_ROWS = 16
LANES = 128
CONV_ACC_VREGS = 16
VMEM_LIMIT_BYTES = 56 * 1024 * 1024

BF16 = jnp.bfloat16
F32 = jnp.float32


def _dot(a, b):
    return jnp.dot(a, b, preferred_element_type=F32)


def _rmsnorm(x, g):
    return x * lax.rsqrt(jnp.mean(x * x, axis=-1, keepdims=True) + EPS) * g


def _layernorm(x, g, b):
    mu = jnp.mean(x, axis=-1, keepdims=True)
    xc = x - mu
    return xc * lax.rsqrt(jnp.mean(xc * xc, axis=-1, keepdims=True) + EPS) * g + b


def _swiglu_half_step(x, g, wg_ref, wu_ref, wd_ref):
    h = _rmsnorm(x, g).astype(BF16)
    act = jax.nn.silu(_dot(h, wg_ref[...])) * _dot(h, wu_ref[...])
    return x + 0.5 * _dot(act.astype(BF16), wd_ref[...])


def _two_groups(n_prompt_steps, prompt_body, sample_body):
    step = pl.program_id(0)
    pl.when(step < n_prompt_steps)(prompt_body)
    pl.when(step == n_prompt_steps)(sample_body)


def _ffn_kernel(n_prompt_steps, xp_ref, xs_ref, g_ref, wg_ref, wu_ref, wd_ref, op_ref, os_ref):
    def tile(x_ref, o_ref):
        o_ref[...] = _swiglu_half_step(x_ref[...], g_ref[...], wg_ref, wu_ref, wd_ref)

    _two_groups(n_prompt_steps, functools.partial(tile, xp_ref, op_ref), functools.partial(tile, xs_ref, os_ref))


def _ffn_ple_kernel(final, n_prompt_steps, xp_ref, xs_ref, pp_ref, ps_ref, g_ref, wg_ref, wu_ref, wd_ref,
                    pg_ref, wpg_ref, wpp_ref, fg_ref, op_ref, os_ref):
    def tile(x_ref, p_ref, o_ref):
        x = _swiglu_half_step(x_ref[...], g_ref[...], wg_ref, wu_ref, wd_ref)
        gate = jax.nn.sigmoid(_dot(_rmsnorm(x, pg_ref[...]).astype(BF16), wpg_ref[...]))
        x = x + gate * _dot(p_ref[...].astype(BF16), wpp_ref[...])
        if final:
            x = _rmsnorm(x, fg_ref[...])
        o_ref[...] = x

    _two_groups(n_prompt_steps, functools.partial(tile, xp_ref, pp_ref, op_ref),
                functools.partial(tile, xs_ref, ps_ref, os_ref))


def _causal_dwconv(zp_ref, zsh_ref, w_ref, n_taps, first_row, n_seq, n_rows):
    if zsh_ref is not None:
        span = zsh_ref.shape[2]
        for r in range(1, SUBLANES):
            zsh_ref[r - 1] = zp_ref[:, pl.ds(r, span), :]
    rows = min(n_rows, 32)
    seqs = max(1, min(n_seq, CONV_ACC_VREGS * SUBLANES * LANES // (rows * W_BR)))
    w = w_ref[...]
    seq_parts = []
    for s0 in range(0, n_seq, seqs):
        row_parts = []
        for r0 in range(0, n_rows, rows):
            acc = jnp.zeros((seqs, rows, W_BR), F32)
            for k in range(n_taps):
                shift, base = (first_row + k) % SUBLANES, (first_row + k) // SUBLANES * SUBLANES
                if zsh_ref is None:
                    win = zp_ref[pl.ds(s0, seqs), pl.ds(first_row + r0 + k, rows), :]
                elif shift == 0:
                    win = zp_ref[pl.ds(s0, seqs), pl.ds(base + r0, rows), :]
                else:
                    win = zsh_ref[shift - 1, pl.ds(s0, seqs), pl.ds(base + r0, rows), :]
                acc = acc + w[k] * win
            row_parts.append(acc)
        seq_parts.append(row_parts[0] if len(row_parts) == 1 else jnp.concatenate(row_parts, axis=1))
    return seq_parts[0] if len(seq_parts) == 1 else jnp.concatenate(seq_parts, axis=0)


def _spatial_mix(vn, wsp_ref, bsp_ref, n_seq, n_rows):
    t = min(n_rows, MLP_CHUNK)
    r_total = n_seq * n_rows
    tril = lax.broadcasted_iota(jnp.int32, (t, t), 0) >= lax.broadcasted_iota(jnp.int32, (t, t), 1)
    heads = []
    if t == MLP_CHUNK:
        n_chunks = r_total // t
        bias = bsp_ref[...]
        for h in range(H_C):
            cols = slice(h * C_HEAD, (h + 1) * C_HEAD)
            wm = jnp.where(tril, wsp_ref[h], 0.0).astype(BF16)
            rhs = jnp.concatenate([vn[j * t:(j + 1) * t, cols] for j in range(n_chunks)], axis=1)
            mh = _dot(wm, rhs) + bias[:, h:h + 1]
            heads.append(jnp.concatenate([mh[:, j * C_HEAD:(j + 1) * C_HEAD] for j in range(n_chunks)], axis=0))
    else:
        sel = (lax.broadcasted_iota(jnp.int32, (r_total, t), 0) % t
               == lax.broadcasted_iota(jnp.int32, (r_total, t), 1)).astype(F32)
        sel_t = (lax.broadcasted_iota(jnp.int32, (t, r_total), 1) % t
                 == lax.broadcasted_iota(jnp.int32, (t, r_total), 0)).astype(F32)
        same_seq = (lax.broadcasted_iota(jnp.int32, (r_total, r_total), 0) // t
                    == lax.broadcasted_iota(jnp.int32, (r_total, r_total), 1) // t)
        bias = jnp.concatenate([bsp_ref[0:t, :]] * n_seq, axis=0)
        for h in range(H_C):
            cols = slice(h * C_HEAD, (h + 1) * C_HEAD)
            wm = jnp.where(tril, wsp_ref[h, 0:t, 0:t], 0.0)
            big = jnp.where(same_seq, _dot(_dot(sel, wm), sel_t), 0.0).astype(BF16)
            heads.append(_dot(big, vn[:, cols]) + bias[:, h:h + 1])
    return jnp.concatenate(heads, axis=1)


def _mixer_tile(n_seq, n_rows, fresh, x_ref, hista_ref, histb_ref, params, o_ref, newa_ref, newb_ref,
                vn_ref, zpa_ref, zpb_ref, zsh_ref, h_ref, proj_ref):
    (g_ref, win_ref, gbias_ref, caw_ref, cab_ref, lnag_ref, lnab_ref, waout_ref, cbw_ref, wbout_ref,
     lncg_ref, lncb_ref, wsp_ref, bsp_ref, wcout_ref, wo_ref) = params
    r_total = n_seq * n_rows

    @pl.when(fresh)
    def _():
        zpa_ref[:, 0:HIST_A_PAD, :] = hista_ref[...]
        zpb_ref[:, 0:HIST_B_PAD, :] = histb_ref[...]

    h_ref[...] = _rmsnorm(x_ref[...], g_ref[...]).astype(BF16)

    proj_ref[:, COL_A:COL_B] = _dot(h_ref[...], win_ref[:, COL_A:COL_B])
    za = proj_ref[:, 0:W_BR] * jax.nn.sigmoid(proj_ref[:, W_BR:COL_B])
    zpa_ref[:, HIST_A_PAD:HIST_A_PAD + n_rows, :] = za.reshape(n_seq, n_rows, W_BR)

    proj_ref[:, COL_B:COL_G] = _dot(h_ref[...], win_ref[:, COL_B:COL_G])
    ca = _causal_dwconv(zpa_ref, zsh_ref, caw_ref, CONV_A, HIST_A_PAD - (CONV_A - 1), n_seq, n_rows)
    ca = ca.reshape(r_total, W_BR) + cab_ref[...]
    ya_in = jax.nn.silu(_layernorm(ca, lnag_ref[...], lnab_ref[...])).astype(BF16)
    tail_a = zpa_ref[:, n_rows:n_rows + HIST_A_PAD, :]
    zpa_ref[:, 0:HIST_A_PAD, :] = tail_a
    newa_ref[...] = tail_a

    proj_ref[:, COL_G:COL_END] = _dot(h_ref[...], win_ref[:, COL_G:COL_END])

    zb = proj_ref[:, COL_B + W_BR:COL_B + 2 * W_BR] * proj_ref[:, COL_B + 2 * W_BR:COL_C]
    zpb_ref[:, HIST_B_PAD:HIST_B_PAD + n_rows, :] = zb.reshape(n_seq, n_rows, W_BR)
    cb = _causal_dwconv(zpb_ref, None, cbw_ref, CONV_B, HIST_B_PAD - (CONV_B - 1), n_seq, n_rows)
    yb_in = (proj_ref[:, COL_B:COL_B + W_BR] * cb.reshape(r_total, W_BR)).astype(BF16)
    tail_b = zpb_ref[:, n_rows:n_rows + HIST_B_PAD, :]
    zpb_ref[:, 0:HIST_B_PAD, :] = tail_b
    newb_ref[...] = tail_b

    vn = _layernorm(jax.nn.gelu(proj_ref[:, COL_C + W_BR:COL_G]), lncg_ref[...], lncb_ref[...])
    if vn_ref is not None:
        vn_ref[...] = vn
    gu = jax.nn.gelu(proj_ref[:, COL_C:COL_C + W_BR])

    ya = _dot(ya_in, waout_ref[...])
    yb = _dot(yb_in, wbout_ref[...])
    mixed = _spatial_mix(vn.astype(BF16), wsp_ref, bsp_ref, n_seq, n_rows)
    yc = _dot((gu * mixed).astype(BF16), wcout_ref[...])

    gates = jax.nn.sigmoid(proj_ref[:, COL_G:COL_END] + gbias_ref[...])
    merged = (gates[:, :D_MODEL] * ya + gates[:, D_MODEL:2 * D_MODEL] * yb
              + gates[:, 2 * D_MODEL:] * yc)
    o_ref[...] = x_ref[...] + _dot(merged.astype(BF16), wo_ref[...])


N_MIXER_PARAMS = 16


def _mixer_kernel(n_seq, n_rows, tiles_per_seq, emit_v, x_ref, hista_ref, histb_ref, *refs):
    params = refs[:N_MIXER_PARAMS]
    o_ref, newa_ref, newb_ref = refs[N_MIXER_PARAMS:N_MIXER_PARAMS + 3]
    rest = refs[N_MIXER_PARAMS + 3:]
    vn_ref = rest[0] if emit_v else None
    step = pl.program_id(0)
    pl.when(step < pl.num_programs(0) - 1)(
        lambda: _mixer_tile(n_seq, n_rows, step % tiles_per_seq == 0, x_ref, hista_ref, histb_ref, params,
                            o_ref, newa_ref, newb_ref, vn_ref, *rest[1 if emit_v else 0:]))


class _Cast(NamedTuple):
    src: jax.Array
    layer: int


def _cast_chunks(rows, n_steps):
    units = rows // BF16_ROWS
    n_chunks = max(d for d in range(1, min(units, n_steps) + 1) if units % d == 0)
    return rows // n_chunks, n_chunks


def _with_casts(body, n_in, n_out, n_casts, *refs):
    ins, cast_ins = refs[:n_in], refs[n_in:n_in + n_casts]
    outs = refs[n_in + n_casts:n_in + n_casts + n_out]
    cast_outs = refs[n_in + n_casts + n_out:n_in + 2 * n_casts + n_out]
    body(*ins, *outs, *refs[n_in + 2 * n_casts + n_out:])
    for src_ref, dst_ref in zip(cast_ins, cast_outs):
        dst_ref[...] = src_ref[...].astype(BF16)


def _fused_call(body, name, grid, operands, in_specs, out_shapes, out_specs, scratch, casts):
    n_steps = 1
    for g in grid:
        n_steps *= g
    linear = (lambda t: t) if len(grid) == 1 else (lambda b, t: b * grid[1] + t)
    c_in, c_out, c_shapes = [], [], []
    for c in casts:
        _, rows, cols = c.src.shape
        chunk, n_chunks = _cast_chunks(rows, n_steps)
        at = lambda *g, n=n_chunks: jnp.minimum(linear(*g), n - 1)
        c_in.append(pl.BlockSpec((None, chunk, cols), lambda *g, at=at, layer=c.layer: (layer, at(*g), 0)))
        c_out.append(pl.BlockSpec((chunk, cols), lambda *g, at=at: (at(*g), 0)))
        c_shapes.append(jax.ShapeDtypeStruct((rows, cols), BF16))
    outs = pl.pallas_call(
        functools.partial(_with_casts, body, len(operands), len(out_shapes), len(casts)),
        out_shape=tuple(out_shapes) + tuple(c_shapes),
        grid=grid,
        in_specs=list(in_specs) + c_in,
        out_specs=tuple(out_specs) + tuple(c_out),
        scratch_shapes=scratch,
        compiler_params=pltpu.CompilerParams(dimension_semantics=("arbitrary",) * len(grid),
                                             vmem_limit_bytes=VMEM_LIMIT_BYTES),
        name=name,
    )(*operands, *[c.src for c in casts])
    return outs[:len(out_shapes)], outs[len(out_shapes):]


def _layer_spec(layer, tail_shape):
    zeros = (0,) * len(tail_shape)
    return pl.BlockSpec((None,) + tuple(tail_shape), lambda *g: (layer,) + zeros,
                        pipeline_mode=pl.Buffered(1))


def _resident_spec(shape):
    return pl.BlockSpec(tuple(shape), lambda *g: (0, 0), pipeline_mode=pl.Buffered(1))


def _tile_spec(rows, width, n_prompt_steps):
    return pl.BlockSpec((rows, width), lambda t: (jnp.minimum(t, n_prompt_steps - 1), 0))


def _whole_spec(shape):
    return pl.BlockSpec(tuple(shape), lambda t: (0,) * len(shape))


def _ffn_call(xp, xs, layer, w, wb, tm, casts):
    n_p = xp.shape[0] // tm
    outs, cast_out = _fused_call(
        functools.partial(_ffn_kernel, n_p), "ffn1", (n_p + 1,),
        [xp, xs, w['ffn1_norm'], wb['ffn1_w_gate'], wb['ffn1_w_up'], wb['ffn1_w_down']],
        [_tile_spec(tm, D_MODEL, n_p), _whole_spec(xs.shape), _layer_spec(layer, (1, D_MODEL)),
         _resident_spec((D_MODEL, D_FF)), _resident_spec((D_MODEL, D_FF)), _resident_spec((D_FF, D_MODEL))],
        [jax.ShapeDtypeStruct(xp.shape, F32), jax.ShapeDtypeStruct(xs.shape, F32)],
        [_tile_spec(tm, D_MODEL, n_p), _whole_spec(xs.shape)], [], casts)
    return outs, cast_out


def _ffn_ple_call(xp, xs, pp, ps, layer, w, wb, tm, final, casts):
    n_p = xp.shape[0] // tm
    outs, cast_out = _fused_call(
        functools.partial(_ffn_ple_kernel, final, n_p), "ffn2_ple", (n_p + 1,),
        [xp, xs, pp, ps, w['ffn2_norm'], wb['ffn2_w_gate'], wb['ffn2_w_up'], wb['ffn2_w_down'],
         w['ple_norm'], wb['w_ple_gate'], wb['w_ple_proj'], w['final_norm']],
        [_tile_spec(tm, D_MODEL, n_p), _whole_spec(xs.shape),
         pl.BlockSpec((None, tm, D_PLE), lambda t: (layer, jnp.minimum(t, n_p - 1), 0)),
         pl.BlockSpec((None,) + ps.shape[1:], lambda t: (layer, 0, 0)),
         _layer_spec(layer, (1, D_MODEL)), _resident_spec((D_MODEL, D_FF)),
         _resident_spec((D_MODEL, D_FF)), _resident_spec((D_FF, D_MODEL)),
         _layer_spec(layer, (1, D_MODEL)), _resident_spec((D_MODEL, D_MODEL)),
         _resident_spec((D_PLE, D_MODEL)), _resident_spec((1, D_MODEL))],
        [jax.ShapeDtypeStruct(xp.shape, F32), jax.ShapeDtypeStruct(xs.shape, F32)],
        [_tile_spec(tm, D_MODEL, n_p), _whole_spec(xs.shape)], [], casts)
    return outs, cast_out


def _mixer_call(x, hist_a, hist_b, layer, w, wb, seq, tile_rows, emit_v, casts):
    rows = min(seq, tile_rows)
    seqs = tile_rows // rows
    tiles_per_seq = seq // rows
    n_steps = x.shape[0] // tile_rows
    ls = functools.partial(_layer_spec, layer)
    tile_of = lambda t: jnp.minimum(t, n_steps - 1)
    x_spec = lambda width: pl.BlockSpec((tile_rows, width), lambda t: (tile_of(t), 0))
    state = lambda pad: pl.BlockSpec((seqs, pad, W_BR), lambda t: (tile_of(t) // tiles_per_seq, 0, 0))
    in_specs = [
        x_spec(D_MODEL), state(HIST_A_PAD), state(HIST_B_PAD),
        ls((1, D_MODEL)), _resident_spec((D_MODEL, COL_END)), ls((1, COL_END - COL_G)),
        ls((CONV_A, W_BR)), ls((1, W_BR)), ls((1, W_BR)), ls((1, W_BR)), _resident_spec((W_BR, D_MODEL)),
        ls((CONV_B, W_BR)), _resident_spec((W_BR, D_MODEL)),
        ls((1, W_BR)), ls((1, W_BR)), ls((H_C, MLP_CHUNK, MLP_CHUNK)), ls((MLP_CHUNK, H_C)),
        _resident_spec((W_BR, D_MODEL)), _resident_spec((D_MODEL, D_MODEL)),
    ]
    f32 = lambda *shape: jax.ShapeDtypeStruct(shape, F32)
    out_shapes = [f32(*x.shape), f32(*hist_a.shape), f32(*hist_b.shape)]
    out_specs = [x_spec(D_MODEL), state(HIST_A_PAD), state(HIST_B_PAD)]
    if emit_v:
        out_shapes.append(f32(x.shape[0], W_BR))
        out_specs.append(x_spec(W_BR))
    scratch = [pltpu.VMEM((seqs, HIST_A_PAD + rows, W_BR), F32),
               pltpu.VMEM((seqs, HIST_B_PAD + rows, W_BR), F32),
               pltpu.VMEM((SUBLANES - 1, seqs, HIST_A_PAD - SUBLANES + rows, W_BR), F32),
               pltpu.VMEM((tile_rows, D_MODEL), BF16), pltpu.VMEM((tile_rows, COL_END), F32)]
    params = [w['mix_norm'], wb['w_in'], w['gate_bias'],
              w['conv_a_w'], w['conv_a_b'], w['ln_a_g'], w['ln_a_b'], wb['w_a_out'],
              w['conv_b_w'], wb['w_b_out'], w['ln_c_g'], w['ln_c_b'], w['w_spatial'], w['b_spatial'],
              wb['w_c_out'], wb['w_o']]
    assert len(params) == N_MIXER_PARAMS
    outs, cast_out = _fused_call(
        functools.partial(_mixer_kernel, seqs, rows, tiles_per_seq, emit_v), "mixer",
        (n_steps + 1,), [x, hist_a, hist_b] + params, in_specs, out_shapes, out_specs, scratch, casts)
    return (*outs, None)[:4], cast_out


FFN1_WEIGHTS = ('ffn1_w_gate', 'ffn1_w_up', 'ffn1_w_down')
MIXER_WEIGHTS = ('w_in', 'w_a_out', 'w_b_out', 'w_c_out', 'w_o')
FFN2_WEIGHTS = ('ffn2_w_gate', 'ffn2_w_up', 'ffn2_w_down', 'w_ple_gate', 'w_ple_proj')
CAST_ONLY_STEPS = 16


MIXER_TILE_ROWS = 256
FFN_TILE_ROWS = 512


def _trunk(xp, xs, pp, ps, hist_ap, hist_as, hist_bp, hist_bs, w, raw, depth, seq, dec_seq):
    casts = lambda names, layer: [_Cast(raw[n], layer) for n in names] if layer < depth else []
    wb = [dict() for _ in range(depth)]
    _, done = _fused_call(lambda: None, "cast", (CAST_ONLY_STEPS,), [], [], [], [], [], casts(FFN1_WEIGHTS, 0))
    wb[0].update(zip(FFN1_WEIGHTS, done))
    states, chunk_v = [], []
    for i in range(depth):
        (xp, xs), done = _ffn_call(xp, xs, i, w, wb[i], FFN_TILE_ROWS, casts(MIXER_WEIGHTS, i))
        wb[i].update(zip(MIXER_WEIGHTS, done))
        (xp, nap, nbp, _), done = _mixer_call(xp, hist_ap[i], hist_bp[i], i, w, wb[i], seq, MIXER_TILE_ROWS,
                                              False, casts(FFN2_WEIGHTS, i))
        wb[i].update(zip(FFN2_WEIGHTS, done))
        (xs, nas, nbs, vn), _ = _mixer_call(xs, hist_as[i], hist_bs[i], i, w, wb[i], dec_seq, MIXER_TILE_ROWS,
                                            True, [])
        (xp, xs), done = _ffn_ple_call(xp, xs, pp, ps, i, w, wb[i], FFN_TILE_ROWS, i == depth - 1,
                                       casts(FFN1_WEIGHTS, i + 1))
        if done:
            wb[i + 1].update(zip(FFN1_WEIGHTS, done))
        states.append([nap[:, HIST_A_PAD - (CONV_A - 1):], nas[:, HIST_A_PAD - (CONV_A - 1):],
                       nbp[:, HIST_B_PAD - (CONV_B - 1):], nbs[:, HIST_B_PAD - (CONV_B - 1):]])
        chunk_v.append(vn)
    return (xp, xs, *[jnp.stack(s) for s in zip(*states)], jnp.stack(chunk_v))


def _front_pad(hist, pad_to):
    return jnp.pad(hist, ((0, 0), (0, 0), (pad_to - hist.shape[2], 0), (0, 0)))


def kernel(x_prompt, x_sample, p_prompt, p_sample, cache_conv_a, cache_conv_b, ffn1_norm, ffn1_w_gate, ffn1_w_up, ffn1_w_down, mix_norm, w_in, gate_bias, conv_a_w, conv_a_b, ln_a_g, ln_a_b, w_a_out, conv_b_w, w_b_out, ln_c_g, ln_c_b, w_spatial, b_spatial, w_c_out, w_o, ffn2_norm, ffn2_w_gate, ffn2_w_up, ffn2_w_down, ple_norm, w_ple_gate, w_ple_proj, final_norm):
    depth = w_in.shape[0]
    batch, seq, _ = x_prompt.shape
    dec_batch, dec_seq, _ = x_sample.shape
    row = lambda a: a.reshape(a.shape[0], 1, a.shape[1])
    w = dict(
        ffn1_norm=row(ffn1_norm), mix_norm=row(mix_norm), gate_bias=row(gate_bias),
        conv_a_w=conv_a_w, conv_a_b=row(conv_a_b), ln_a_g=row(ln_a_g), ln_a_b=row(ln_a_b),
        conv_b_w=conv_b_w, ln_c_g=row(ln_c_g), ln_c_b=row(ln_c_b), w_spatial=w_spatial,
        b_spatial=jnp.swapaxes(b_spatial, 1, 2), ffn2_norm=row(ffn2_norm), ple_norm=row(ple_norm),
        final_norm=final_norm.reshape(1, D_MODEL))
    raw = dict(ffn1_w_gate=ffn1_w_gate, ffn1_w_up=ffn1_w_up, ffn1_w_down=ffn1_w_down,
               w_in=w_in, w_a_out=w_a_out, w_b_out=w_b_out, w_c_out=w_c_out, w_o=w_o,
               ffn2_w_gate=ffn2_w_gate, ffn2_w_up=ffn2_w_up, ffn2_w_down=ffn2_w_down,
               w_ple_gate=w_ple_gate, w_ple_proj=w_ple_proj)

    (y_prompt, y_sample, conv_a_prompt, conv_a_sample, conv_b_prompt, conv_b_sample, chunk_v) = _trunk(
        x_prompt.reshape(batch * seq, D_MODEL), x_sample.reshape(dec_batch * dec_seq, D_MODEL),
        p_prompt.reshape(depth, batch * seq, D_PLE), p_sample.reshape(depth, dec_batch * dec_seq, D_PLE),
        jnp.zeros((depth, batch, HIST_A_PAD, W_BR), F32), _front_pad(cache_conv_a, HIST_A_PAD),
        jnp.zeros((depth, batch, HIST_B_PAD, W_BR), F32), _front_pad(cache_conv_b, HIST_B_PAD),
        w, raw, depth, seq, dec_seq)

    return (y_prompt.reshape(x_prompt.shape), y_sample.reshape(x_sample.shape),
            conv_a_prompt, conv_a_sample, conv_b_prompt, conv_b_sample,
            chunk_v.reshape(depth, dec_batch, dec_seq, W_BR))
```

```python
import functools
from typing import NamedTuple

import jax
import jax.numpy as jnp
from jax import lax
from jax.experimental import pallas as pl
from jax.experimental.pallas import tpu as pltpu

D_MODEL = 1024
D_FF = 2816
D_PLE = 256
W_BR = 512
CONV_A = 31
CONV_B = 3
H_C = 4
C_HEAD = W_BR // H_C
MLP_CHUNK = 128
EPS = 1e-6
COL_A, COL_B, COL_C, COL_G, COL_END = 0, 1024, 2560, 3584, 6656

HIST_A_PAD = 32
HIST_B_PAD = 8
SUBLANES = 8
BF16_ROWS = 16
LANES = 128
CONV_ACC_VREGS = 16
VMEM_LIMIT_BYTES = 48 * 1024 * 1024

BF16 = jnp.bfloat16
F32 = jnp.float32


def _dot(a, b):
    return jnp.dot(a, b, preferred_element_type=F32)


def _rmsnorm(x, g):
    return x * lax.rsqrt(jnp.mean(x * x, axis=-1, keepdims=True) + EPS) * g


def _layernorm(x, g, b):
    mu = jnp.mean(x, axis=-1, keepdims=True)
    xc = x - mu
    return xc * lax.rsqrt(jnp.mean(xc * xc, axis=-1, keepdims=True) + EPS) * g + b


def _swiglu_half_step(x, g, wg_ref, wu_ref, wd_ref):
    h = _rmsnorm(x, g).astype(BF16)
    act = jax.nn.silu(_dot(h, wg_ref[...])) * _dot(h, wu_ref[...])
    return x + 0.5 * _dot(act.astype(BF16), wd_ref[...])


def _two_groups(n_prompt_steps, prompt_body, sample_body):
    step = pl.program_id(0)
    pl.when(step < n_prompt_steps)(prompt_body)
    pl.when(step == n_prompt_steps)(sample_body)


def _ffn_kernel(n_prompt_steps, xp_ref, xs_ref, g_ref, wg_ref, wu_ref, wd_ref, op_ref, os_ref):
    def tile(x_ref, o_ref):
        o_ref[...] = _swiglu_half_step(x_ref[...], g_ref[...], wg_ref, wu_ref, wd_ref)

    _two_groups(n_prompt_steps, functools.partial(tile, xp_ref, op_ref), functools.partial(tile, xs_ref, os_ref))


def _ffn_ple_kernel(final, n_prompt_steps, xp_ref, xs_ref, pp_ref, ps_ref, g_ref, wg_ref, wu_ref, wd_ref,
                    pg_ref, wpg_ref, wpp_ref, fg_ref, op_ref, os_ref):
    def tile(x_ref, p_ref, o_ref):
        x = _swiglu_half_step(x_ref[...], g_ref[...], wg_ref, wu_ref, wd_ref)
        gate = jax.nn.sigmoid(_dot(_rmsnorm(x, pg_ref[...]).astype(BF16), wpg_ref[...]))
        x = x + gate * _dot(p_ref[...].astype(BF16), wpp_ref[...])
        if final:
            x = _rmsnorm(x, fg_ref[...])
        o_ref[...] = x

    _two_groups(n_prompt_steps, functools.partial(tile, xp_ref, pp_ref, op_ref),
                functools.partial(tile, xs_ref, ps_ref, os_ref))


def _causal_dwconv(zp_ref, zsh_ref, w_ref, n_taps, first_row, n_seq, n_rows):
    if zsh_ref is not None:
        span = zsh_ref.shape[2]
        for r in range(1, SUBLANES):
            zsh_ref[r - 1] = zp_ref[:, pl.ds(r, span), :]
    rows = min(n_rows, 32)
    seqs = max(1, min(n_seq, CONV_ACC_VREGS * SUBLANES * LANES // (rows * W_BR)))
    w = w_ref[...]
    seq_parts = []
    for s0 in range(0, n_seq, seqs):
        row_parts = []
        for r0 in range(0, n_rows, rows):
            acc = jnp.zeros((seqs, rows, W_BR), F32)
            for k in range(n_taps):
                shift, base = (first_row + k) % SUBLANES, (first_row + k) // SUBLANES * SUBLANES
                if zsh_ref is None:
                    win = zp_ref[pl.ds(s0, seqs), pl.ds(first_row + r0 + k, rows), :]
                elif shift == 0:
                    win = zp_ref[pl.ds(s0, seqs), pl.ds(base + r0, rows), :]
                else:
                    win = zsh_ref[shift - 1, pl.ds(s0, seqs), pl.ds(base + r0, rows), :]
                acc = acc + w[k] * win
            row_parts.append(acc)
        seq_parts.append(row_parts[0] if len(row_parts) == 1 else jnp.concatenate(row_parts, axis=1))
    return seq_parts[0] if len(seq_parts) == 1 else jnp.concatenate(seq_parts, axis=0)


def _spatial_mix(vn, wsp_ref, bsp_ref, n_seq, n_rows):
    t = min(n_rows, MLP_CHUNK)
    r_total = n_seq * n_rows
    tril = lax.broadcasted_iota(jnp.int32, (t, t), 0) >= lax.broadcasted_iota(jnp.int32, (t, t), 1)
    heads = []
    if t == MLP_CHUNK:
        n_chunks = r_total // t
        bias = bsp_ref[...]
        for h in range(H_C):
            cols = slice(h * C_HEAD, (h + 1) * C_HEAD)
            wm = jnp.where(tril, wsp_ref[h], 0.0).astype(BF16)
            rhs = jnp.concatenate([vn[j * t:(j + 1) * t, cols] for j in range(n_chunks)], axis=1)
            mh = _dot(wm, rhs) + bias[:, h:h + 1]
            heads.append(jnp.concatenate([mh[:, j * C_HEAD:(j + 1) * C_HEAD] for j in range(n_chunks)], axis=0))
    else:
        sel = (lax.broadcasted_iota(jnp.int32, (r_total, t), 0) % t
               == lax.broadcasted_iota(jnp.int32, (r_total, t), 1)).astype(F32)
        sel_t = (lax.broadcasted_iota(jnp.int32, (t, r_total), 1) % t
                 == lax.broadcasted_iota(jnp.int32, (t, r_total), 0)).astype(F32)
        same_seq = (lax.broadcasted_iota(jnp.int32, (r_total, r_total), 0) // t
                    == lax.broadcasted_iota(jnp.int32, (r_total, r_total), 1) // t)
        bias = jnp.concatenate([bsp_ref[0:t, :]] * n_seq, axis=0)
        for h in range(H_C):
            cols = slice(h * C_HEAD, (h + 1) * C_HEAD)
            wm = jnp.where(tril, wsp_ref[h, 0:t, 0:t], 0.0)
            big = jnp.where(same_seq, _dot(_dot(sel, wm), sel_t), 0.0).astype(BF16)
            heads.append(_dot(big, vn[:, cols]) + bias[:, h:h + 1])
    return jnp.concatenate(heads, axis=1)


def _mixer_tile(n_seq, n_rows, fresh, x_ref, hista_ref, histb_ref, params, o_ref, newa_ref, newb_ref,
                vn_ref, zpa_ref, zpb_ref, zsh_ref, h_ref, proj_ref):
    (g_ref, win_ref, gbias_ref, caw_ref, cab_ref, lnag_ref, lnab_ref, waout_ref, cbw_ref, wbout_ref,
     lncg_ref, lncb_ref, wsp_ref, bsp_ref, wcout_ref, wo_ref) = params
    r_total = n_seq * n_rows

    @pl.when(fresh)
    def _():
        zpa_ref[:, 0:HIST_A_PAD, :] = hista_ref[...]
        zpb_ref[:, 0:HIST_B_PAD, :] = histb_ref[...]

    h_ref[...] = _rmsnorm(x_ref[...], g_ref[...]).astype(BF16)

    proj_ref[:, COL_A:COL_B] = _dot(h_ref[...], win_ref[:, COL_A:COL_B])
    za = proj_ref[:, 0:W_BR] * jax.nn.sigmoid(proj_ref[:, W_BR:COL_B])
    zpa_ref[:, HIST_A_PAD:HIST_A_PAD + n_rows, :] = za.reshape(n_seq, n_rows, W_BR)

    proj_ref[:, COL_B:COL_G] = _dot(h_ref[...], win_ref[:, COL_B:COL_G])
    ca = _causal_dwconv(zpa_ref, zsh_ref, caw_ref, CONV_A, HIST_A_PAD - (CONV_A - 1), n_seq, n_rows)
    ca = ca.reshape(r_total, W_BR) + cab_ref[...]
    ya_in = jax.nn.silu(_layernorm(ca, lnag_ref[...], lnab_ref[...])).astype(BF16)
    tail_a = zpa_ref[:, n_rows:n_rows + HIST_A_PAD, :]
    zpa_ref[:, 0:HIST_A_PAD, :] = tail_a
    newa_ref[...] = tail_a

    proj_ref[:, COL_G:COL_END] = _dot(h_ref[...], win_ref[:, COL_G:COL_END])

    zb = proj_ref[:, COL_B + W_BR:COL_B + 2 * W_BR] * proj_ref[:, COL_B + 2 * W_BR:COL_C]
    zpb_ref[:, HIST_B_PAD:HIST_B_PAD + n_rows, :] = zb.reshape(n_seq, n_rows, W_BR)
    cb = _causal_dwconv(zpb_ref, None, cbw_ref, CONV_B, HIST_B_PAD - (CONV_B - 1), n_seq, n_rows)
    yb_in = (proj_ref[:, COL_B:COL_B + W_BR] * cb.reshape(r_total, W_BR)).astype(BF16)
    tail_b = zpb_ref[:, n_rows:n_rows + HIST_B_PAD, :]
    zpb_ref[:, 0:HIST_B_PAD, :] = tail_b
    newb_ref[...] = tail_b

    vn = _layernorm(jax.nn.gelu(proj_ref[:, COL_C + W_BR:COL_G]), lncg_ref[...], lncb_ref[...])
    if vn_ref is not None:
        vn_ref[...] = vn
    gu = jax.nn.gelu(proj_ref[:, COL_C:COL_C + W_BR])

    ya = _dot(ya_in, waout_ref[...])
    yb = _dot(yb_in, wbout_ref[...])
    mixed = _spatial_mix(vn.astype(BF16), wsp_ref, bsp_ref, n_seq, n_rows)
    yc = _dot((gu * mixed).astype(BF16), wcout_ref[...])

    gates = jax.nn.sigmoid(proj_ref[:, COL_G:COL_END] + gbias_ref[...])
    merged = (gates[:, :D_MODEL] * ya + gates[:, D_MODEL:2 * D_MODEL] * yb
              + gates[:, 2 * D_MODEL:] * yc)
    o_ref[...] = x_ref[...] + _dot(merged.astype(BF16), wo_ref[...])


N_MIXER_PARAMS = 16


def _mixer_kernel(n_seq, n_rows, tiles_per_seq, emit_v, x_ref, hista_ref, histb_ref, *refs):
    params = refs[:N_MIXER_PARAMS]
    o_ref, newa_ref, newb_ref = refs[N_MIXER_PARAMS:N_MIXER_PARAMS + 3]
    rest = refs[N_MIXER_PARAMS + 3:]
    vn_ref = rest[0] if emit_v else None
    _mixer_tile(n_seq, n_rows, pl.program_id(0) % tiles_per_seq == 0, x_ref, hista_ref, histb_ref, params,
                o_ref, newa_ref, newb_ref, vn_ref, *rest[1 if emit_v else 0:])


class _Cast(NamedTuple):
    src: jax.Array
    layer: int


def _cast_chunks(rows, n_steps):
    units = rows // BF16_ROWS
    n_chunks = max(d for d in range(1, min(units, n_steps) + 1) if units % d == 0)
    return rows // n_chunks, n_chunks


def _with_casts(body, n_in, n_out, n_casts, *refs):
    ins, cast_ins = refs[:n_in], refs[n_in:n_in + n_casts]
    outs = refs[n_in + n_casts:n_in + n_casts + n_out]
    cast_outs = refs[n_in + n_casts + n_out:n_in + 2 * n_casts + n_out]
    body(*ins, *outs, *refs[n_in + 2 * n_casts + n_out:])
    for src_ref, dst_ref in zip(cast_ins, cast_outs):
        dst_ref[...] = src_ref[...].astype(BF16)


def _fused_call(body, name, grid, operands, in_specs, out_shapes, out_specs, scratch, casts):
    n_steps = 1
    for g in grid:
        n_steps *= g
    linear = (lambda t: t) if len(grid) == 1 else (lambda b, t: b * grid[1] + t)
    c_in, c_out, c_shapes = [], [], []
    for c in casts:
        _, rows, cols = c.src.shape
        chunk, n_chunks = _cast_chunks(rows, n_steps)
        at = lambda *g, n=n_chunks: jnp.minimum(linear(*g), n - 1)
        c_in.append(pl.BlockSpec((None, chunk, cols), lambda *g, at=at, layer=c.layer: (layer, at(*g), 0)))
        c_out.append(pl.BlockSpec((chunk, cols), lambda *g, at=at: (at(*g), 0)))
        c_shapes.append(jax.ShapeDtypeStruct((rows, cols), BF16))
    outs = pl.pallas_call(
        functools.partial(_with_casts, body, len(operands), len(out_shapes), len(casts)),
        out_shape=tuple(out_shapes) + tuple(c_shapes),
        grid=grid,
        in_specs=list(in_specs) + c_in,
        out_specs=tuple(out_specs) + tuple(c_out),
        scratch_shapes=scratch,
        compiler_params=pltpu.CompilerParams(dimension_semantics=("arbitrary",) * len(grid),
                                             vmem_limit_bytes=VMEM_LIMIT_BYTES),
        name=name,
    )(*operands, *[c.src for c in casts])
    return outs[:len(out_shapes)], outs[len(out_shapes):]


def _layer_spec(layer, tail_shape):
    zeros = (0,) * len(tail_shape)
    return pl.BlockSpec((None,) + tuple(tail_shape), lambda *g: (layer,) + zeros,
                        pipeline_mode=pl.Buffered(1))


def _resident_spec(shape):
    return pl.BlockSpec(tuple(shape), lambda *g: (0, 0), pipeline_mode=pl.Buffered(1))


def _tile_spec(rows, width, n_prompt_steps):
    return pl.BlockSpec((rows, width), lambda t: (jnp.minimum(t, n_prompt_steps - 1), 0))


def _whole_spec(shape):
    return pl.BlockSpec(tuple(shape), lambda t: (0,) * len(shape))


def _ffn_call(xp, xs, layer, w, wb, tm, casts):
    n_p = xp.shape[0] // tm
    outs, cast_out = _fused_call(
        functools.partial(_ffn_kernel, n_p), "ffn1", (n_p + 1,),
        [xp, xs, w['ffn1_norm'], wb['ffn1_w_gate'], wb['ffn1_w_up'], wb['ffn1_w_down']],
        [_tile_spec(tm, D_MODEL, n_p), _whole_spec(xs.shape), _layer_spec(layer, (1, D_MODEL)),
         _resident_spec((D_MODEL, D_FF)), _resident_spec((D_MODEL, D_FF)), _resident_spec((D_FF, D_MODEL))],
        [jax.ShapeDtypeStruct(xp.shape, F32), jax.ShapeDtypeStruct(xs.shape, F32)],
        [_tile_spec(tm, D_MODEL, n_p), _whole_spec(xs.shape)], [], casts)
    return outs, cast_out


def _ffn_ple_call(xp, xs, pp, ps, layer, w, wb, tm, final, casts):
    n_p = xp.shape[0] // tm
    outs, cast_out = _fused_call(
        functools.partial(_ffn_ple_kernel, final, n_p), "ffn2_ple", (n_p + 1,),
        [xp, xs, pp, ps, w['ffn2_norm'], wb['ffn2_w_gate'], wb['ffn2_w_up'], wb['ffn2_w_down'],
         w['ple_norm'], wb['w_ple_gate'], wb['w_ple_proj'], w['final_norm']],
        [_tile_spec(tm, D_MODEL, n_p), _whole_spec(xs.shape),
         pl.BlockSpec((None, tm, D_PLE), lambda t: (layer, jnp.minimum(t, n_p - 1), 0)),
         pl.BlockSpec((None,) + ps.shape[1:], lambda t: (layer, 0, 0)),
         _layer_spec(layer, (1, D_MODEL)), _resident_spec((D_MODEL, D_FF)),
         _resident_spec((D_MODEL, D_FF)), _resident_spec((D_FF, D_MODEL)),
         _layer_spec(layer, (1, D_MODEL)), _resident_spec((D_MODEL, D_MODEL)),
         _resident_spec((D_PLE, D_MODEL)), _resident_spec((1, D_MODEL))],
        [jax.ShapeDtypeStruct(xp.shape, F32), jax.ShapeDtypeStruct(xs.shape, F32)],
        [_tile_spec(tm, D_MODEL, n_p), _whole_spec(xs.shape)], [], casts)
    return outs, cast_out


def _mixer_call(x, hist_a, hist_b, layer, w, wb, seq, tile_rows, emit_v, casts):
    rows = min(seq, tile_rows)
    seqs = tile_rows // rows
    tiles_per_seq = seq // rows
    n_steps = x.shape[0] // tile_rows
    ls = functools.partial(_layer_spec, layer)
    x_spec = lambda width: pl.BlockSpec((tile_rows, width), lambda t: (t, 0))
    state = lambda pad: pl.BlockSpec((seqs, pad, W_BR), lambda t: (t // tiles_per_seq, 0, 0))
    in_specs = [
        x_spec(D_MODEL), state(HIST_A_PAD), state(HIST_B_PAD),
        ls((1, D_MODEL)), _resident_spec((D_MODEL, COL_END)), ls((1, COL_END - COL_G)),
        ls((CONV_A, W_BR)), ls((1, W_BR)), ls((1, W_BR)), ls((1, W_BR)), _resident_spec((W_BR, D_MODEL)),
        ls((CONV_B, W_BR)), _resident_spec((W_BR, D_MODEL)),
        ls((1, W_BR)), ls((1, W_BR)), ls((H_C, MLP_CHUNK, MLP_CHUNK)), ls((MLP_CHUNK, H_C)),
        _resident_spec((W_BR, D_MODEL)), _resident_spec((D_MODEL, D_MODEL)),
    ]
    f32 = lambda *shape: jax.ShapeDtypeStruct(shape, F32)
    out_shapes = [f32(*x.shape), f32(*hist_a.shape), f32(*hist_b.shape)]
    out_specs = [x_spec(D_MODEL), state(HIST_A_PAD), state(HIST_B_PAD)]
    if emit_v:
        out_shapes.append(f32(x.shape[0], W_BR))
        out_specs.append(x_spec(W_BR))
    scratch = [pltpu.VMEM((seqs, HIST_A_PAD + rows, W_BR), F32),
               pltpu.VMEM((seqs, HIST_B_PAD + rows, W_BR), F32),
               pltpu.VMEM((SUBLANES - 1, seqs, HIST_A_PAD - SUBLANES + rows, W_BR), F32),
               pltpu.VMEM((tile_rows, D_MODEL), BF16), pltpu.VMEM((tile_rows, COL_END), F32)]
    params = [w['mix_norm'], wb['w_in'], w['gate_bias'],
              w['conv_a_w'], w['conv_a_b'], w['ln_a_g'], w['ln_a_b'], wb['w_a_out'],
              w['conv_b_w'], wb['w_b_out'], w['ln_c_g'], w['ln_c_b'], w['w_spatial'], w['b_spatial'],
              wb['w_c_out'], wb['w_o']]
    assert len(params) == N_MIXER_PARAMS
    outs, cast_out = _fused_call(
        functools.partial(_mixer_kernel, seqs, rows, tiles_per_seq, emit_v), "mixer",
        (n_steps,), [x, hist_a, hist_b] + params, in_specs, out_shapes, out_specs, scratch, casts)
    return (*outs, None)[:4], cast_out


FFN1_WEIGHTS = ('ffn1_w_gate', 'ffn1_w_up', 'ffn1_w_down')
MIXER_WEIGHTS = ('w_in', 'w_a_out', 'w_b_out', 'w_c_out', 'w_o')
FFN2_WEIGHTS = ('ffn2_w_gate', 'ffn2_w_up', 'ffn2_w_down', 'w_ple_gate', 'w_ple_proj')
CAST_ONLY_STEPS = 16


MIXER_TILE_ROWS = 256
FFN_TILE_ROWS = 512


def _trunk(xp, xs, pp, ps, hist_ap, hist_as, hist_bp, hist_bs, w, raw, depth, seq, dec_seq):
    casts = lambda names, layer: [_Cast(raw[n], layer) for n in names] if layer < depth else []
    wb = [dict() for _ in range(depth)]
    _, done = _fused_call(lambda: None, "cast", (CAST_ONLY_STEPS,), [], [], [], [], [], casts(FFN1_WEIGHTS, 0))
    wb[0].update(zip(FFN1_WEIGHTS, done))
    states, chunk_v = [], []
    for i in range(depth):
        (xp, xs), done = _ffn_call(xp, xs, i, w, wb[i], FFN_TILE_ROWS, casts(MIXER_WEIGHTS, i))
        wb[i].update(zip(MIXER_WEIGHTS, done))
        (xp, nap, nbp, _), done = _mixer_call(xp, hist_ap[i], hist_bp[i], i, w, wb[i], seq, MIXER_TILE_ROWS,
                                              False, casts(FFN2_WEIGHTS, i))
        wb[i].update(zip(FFN2_WEIGHTS, done))
        (xs, nas, nbs, vn), _ = _mixer_call(xs, hist_as[i], hist_bs[i], i, w, wb[i], dec_seq, MIXER_TILE_ROWS,
                                            True, [])
        (xp, xs), done = _ffn_ple_call(xp, xs, pp, ps, i, w, wb[i], FFN_TILE_ROWS, i == depth - 1,
                                       casts(FFN1_WEIGHTS, i + 1))
        if done:
            wb[i + 1].update(zip(FFN1_WEIGHTS, done))
        states.append([nap[:, HIST_A_PAD - (CONV_A - 1):], nas[:, HIST_A_PAD - (CONV_A - 1):],
                       nbp[:, HIST_B_PAD - (CONV_B - 1):], nbs[:, HIST_B_PAD - (CONV_B - 1):]])
        chunk_v.append(vn)
    return (xp, xs, *[jnp.stack(s) for s in zip(*states)], jnp.stack(chunk_v))


def _front_pad(hist, pad_to):
    return jnp.pad(hist, ((0, 0), (0, 0), (pad_to - hist.shape[2], 0), (0, 0)))


def kernel(x_prompt, x_sample, p_prompt, p_sample, cache_conv_a, cache_conv_b, ffn1_norm, ffn1_w_gate, ffn1_w_up, ffn1_w_down, mix_norm, w_in, gate_bias, conv_a_w, conv_a_b, ln_a_g, ln_a_b, w_a_out, conv_b_w, w_b_out, ln_c_g, ln_c_b, w_spatial, b_spatial, w_c_out, w_o, ffn2_norm, ffn2_w_gate, ffn2_w_up, ffn2_w_down, ple_norm, w_ple_gate, w_ple_proj, final_norm):
    depth = w_in.shape[0]
    batch, seq, _ = x_prompt.shape
    dec_batch, dec_seq, _ = x_sample.shape
    row = lambda a: a.reshape(a.shape[0], 1, a.shape[1])
    w = dict(
        ffn1_norm=row(ffn1_norm), mix_norm=row(mix_norm), gate_bias=row(gate_bias),
        conv_a_w=conv_a_w, conv_a_b=row(conv_a_b), ln_a_g=row(ln_a_g), ln_a_b=row(ln_a_b),
        conv_b_w=conv_b_w, ln_c_g=row(ln_c_g), ln_c_b=row(ln_c_b), w_spatial=w_spatial,
        b_spatial=jnp.swapaxes(b_spatial, 1, 2), ffn2_norm=row(ffn2_norm), ple_norm=row(ple_norm),
        final_norm=final_norm.reshape(1, D_MODEL))
    raw = dict(ffn1_w_gate=ffn1_w_gate, ffn1_w_up=ffn1_w_up, ffn1_w_down=ffn1_w_down,
               w_in=w_in, w_a_out=w_a_out, w_b_out=w_b_out, w_c_out=w_c_out, w_o=w_o,
               ffn2_w_gate=ffn2_w_gate, ffn2_w_up=ffn2_w_up, ffn2_w_down=ffn2_w_down,
               w_ple_gate=w_ple_gate, w_ple_proj=w_ple_proj)

    (y_prompt, y_sample, conv_a_prompt, conv_a_sample, conv_b_prompt, conv_b_sample, chunk_v) = _trunk(
        x_prompt.reshape(batch * seq, D_MODEL), x_sample.reshape(dec_batch * dec_seq, D_MODEL),
        p_prompt.reshape(depth, batch * seq, D_PLE), p_sample.reshape(depth, dec_batch * dec_seq, D_PLE),
        jnp.zeros((depth, batch, HIST_A_PAD, W_BR), F32), _front_pad(cache_conv_a, HIST_A_PAD),
        jnp.zeros((depth, batch, HIST_B_PAD, W_BR), F32), _front_pad(cache_conv_b, HIST_B_PAD),
        w, raw, depth, seq, dec_seq)

    return (y_prompt.reshape(x_prompt.shape), y_sample.reshape(x_sample.shape),
            conv_a_prompt, conv_a_sample, conv_b_prompt, conv_b_sample,
            chunk_v.reshape(depth, dec_batch, dec_seq, W_BR))
```

```python
import functools
from typing import NamedTuple

import jax
import jax.numpy as jnp
from jax import lax
from jax.experimental import pallas as pl
from jax.experimental.pallas import tpu as pltpu

D_MODEL = 1024
D_FF = 2816
D_PLE = 256
W_BR = 512
CONV_A = 31
CONV_B = 3
H_C = 4
C_HEAD = W_BR // H_C
MLP_CHUNK = 128
EPS = 1e-6
COL_A, COL_B, COL_C, COL_G, COL_END = 0, 1024, 2560, 3584, 6656

HIST_A_PAD = 32
HIST_B_PAD = 8
SUBLANES = 8
BF16_ROWS = 16
LANES = 128
CONV_ACC_VREGS = 16
VMEM_LIMIT_BYTES = 56 * 1024 * 1024

BF16 = jnp.bfloat16
F32 = jnp.float32


def _dot(a, b):
    return jnp.dot(a, b, preferred_element_type=F32)


def _rmsnorm(x, g):
    return x * lax.rsqrt(jnp.mean(x * x, axis=-1, keepdims=True) + EPS) * g


def _layernorm(x, g, b):
    mu = jnp.mean(x, axis=-1, keepdims=True)
    xc = x - mu
    return xc * lax.rsqrt(jnp.mean(xc * xc, axis=-1, keepdims=True) + EPS) * g + b


def _swiglu_half_step(x, g, wg_ref, wu_ref, wd_ref):
    h = _rmsnorm(x, g).astype(BF16)
    act = jax.nn.silu(_dot(h, wg_ref[...])) * _dot(h, wu_ref[...])
    return x + 0.5 * _dot(act.astype(BF16), wd_ref[...])


def _two_groups(n_prompt_steps, prompt_body, sample_body):
    step = pl.program_id(0)
    pl.when(step < n_prompt_steps)(prompt_body)
    pl.when(step == n_prompt_steps)(sample_body)


def _ffn_kernel(n_prompt_steps, xp_ref, xs_ref, g_ref, wg_ref, wu_ref, wd_ref, op_ref, os_ref):
    def tile(x_ref, o_ref):
        o_ref[...] = _swiglu_half_step(x_ref[...], g_ref[...], wg_ref, wu_ref, wd_ref)

    _two_groups(n_prompt_steps, functools.partial(tile, xp_ref, op_ref), functools.partial(tile, xs_ref, os_ref))


def _ffn_ple_kernel(final, n_prompt_steps, xp_ref, xs_ref, pp_ref, ps_ref, g_ref, wg_ref, wu_ref, wd_ref,
                    pg_ref, wpg_ref, wpp_ref, fg_ref, op_ref, os_ref):
    def tile(x_ref, p_ref, o_ref):
        x = _swiglu_half_step(x_ref[...], g_ref[...], wg_ref, wu_ref, wd_ref)
        gate = jax.nn.sigmoid(_dot(_rmsnorm(x, pg_ref[...]).astype(BF16), wpg_ref[...]))
        x = x + gate * _dot(p_ref[...].astype(BF16), wpp_ref[...])
        if final:
            x = _rmsnorm(x, fg_ref[...])
        o_ref[...] = x

    _two_groups(n_prompt_steps, functools.partial(tile, xp_ref, pp_ref, op_ref),
                functools.partial(tile, xs_ref, ps_ref, os_ref))


def _causal_dwconv(zp_ref, zsh_ref, w_ref, n_taps, first_row, n_seq, n_rows):
    if zsh_ref is not None:
        span = zsh_ref.shape[2]
        for r in range(1, SUBLANES):
            zsh_ref[r - 1] = zp_ref[:, pl.ds(r, span), :]
    rows = min(n_rows, 32)
    seqs = max(1, min(n_seq, CONV_ACC_VREGS * SUBLANES * LANES // (rows * W_BR)))
    w = w_ref[...]
    seq_parts = []
    for s0 in range(0, n_seq, seqs):
        row_parts = []
        for r0 in range(0, n_rows, rows):
            acc = jnp.zeros((seqs, rows, W_BR), F32)
            for k in range(n_taps):
                shift, base = (first_row + k) % SUBLANES, (first_row + k) // SUBLANES * SUBLANES
                if zsh_ref is None:
                    win = zp_ref[pl.ds(s0, seqs), pl.ds(first_row + r0 + k, rows), :]
                elif shift == 0:
                    win = zp_ref[pl.ds(s0, seqs), pl.ds(base + r0, rows), :]
                else:
                    win = zsh_ref[shift - 1, pl.ds(s0, seqs), pl.ds(base + r0, rows), :]
                acc = acc + w[k] * win
            row_parts.append(acc)
        seq_parts.append(row_parts[0] if len(row_parts) == 1 else jnp.concatenate(row_parts, axis=1))
    return seq_parts[0] if len(seq_parts) == 1 else jnp.concatenate(seq_parts, axis=0)


def _spatial_mix(vn, wsp_ref, bsp_ref, n_seq, n_rows):
    t = min(n_rows, MLP_CHUNK)
    r_total = n_seq * n_rows
    tril = lax.broadcasted_iota(jnp.int32, (t, t), 0) >= lax.broadcasted_iota(jnp.int32, (t, t), 1)
    heads = []
    if t == MLP_CHUNK:
        n_chunks = r_total // t
        bias = bsp_ref[...]
        for h in range(H_C):
            cols = slice(h * C_HEAD, (h + 1) * C_HEAD)
            wm = jnp.where(tril, wsp_ref[h], 0.0).astype(BF16)
            rhs = jnp.concatenate([vn[j * t:(j + 1) * t, cols] for j in range(n_chunks)], axis=1)
            mh = _dot(wm, rhs) + bias[:, h:h + 1]
            heads.append(jnp.concatenate([mh[:, j * C_HEAD:(j + 1) * C_HEAD] for j in range(n_chunks)], axis=0))
    else:
        sel = (lax.broadcasted_iota(jnp.int32, (r_total, t), 0) % t
               == lax.broadcasted_iota(jnp.int32, (r_total, t), 1)).astype(F32)
        sel_t = (lax.broadcasted_iota(jnp.int32, (t, r_total), 1) % t
                 == lax.broadcasted_iota(jnp.int32, (t, r_total), 0)).astype(F32)
        same_seq = (lax.broadcasted_iota(jnp.int32, (r_total, r_total), 0) // t
                    == lax.broadcasted_iota(jnp.int32, (r_total, r_total), 1) // t)
        bias = jnp.concatenate([bsp_ref[0:t, :]] * n_seq, axis=0)
        for h in range(H_C):
            cols = slice(h * C_HEAD, (h + 1) * C_HEAD)
            wm = jnp.where(tril, wsp_ref[h, 0:t, 0:t], 0.0)
            big = jnp.where(same_seq, _dot(_dot(sel, wm), sel_t), 0.0).astype(BF16)
            heads.append(_dot(big, vn[:, cols]) + bias[:, h:h + 1])
    return jnp.concatenate(heads, axis=1)


def _mixer_tile(n_seq, n_rows, fresh, x_ref, hista_ref, histb_ref, params, o_ref, newa_ref, newb_ref,
                vn_ref, zpa_ref, zpb_ref, zsh_ref, h_ref, proj_ref):
    (g_ref, win_ref, gbias_ref, caw_ref, cab_ref, lnag_ref, lnab_ref, waout_ref, cbw_ref, wbout_ref,
     lncg_ref, lncb_ref, wsp_ref, bsp_ref, wcout_ref, wo_ref) = params
    r_total = n_seq * n_rows

    @pl.when(fresh)
    def _():
        zpa_ref[:, 0:HIST_A_PAD, :] = hista_ref[...]
        zpb_ref[:, 0:HIST_B_PAD, :] = histb_ref[...]

    h_ref[...] = _rmsnorm(x_ref[...], g_ref[...]).astype(BF16)

    proj_ref[:, COL_A:COL_B] = _dot(h_ref[...], win_ref[:, COL_A:COL_B])
    za = proj_ref[:, 0:W_BR] * jax.nn.sigmoid(proj_ref[:, W_BR:COL_B])
    zpa_ref[:, HIST_A_PAD:HIST_A_PAD + n_rows, :] = za.reshape(n_seq, n_rows, W_BR)

    proj_ref[:, COL_B:COL_G] = _dot(h_ref[...], win_ref[:, COL_B:COL_G])
    ca = _causal_dwconv(zpa_ref, zsh_ref, caw_ref, CONV_A, HIST_A_PAD - (CONV_A - 1), n_seq, n_rows)
    ca = ca.reshape(r_total, W_BR) + cab_ref[...]
    ya_in = jax.nn.silu(_layernorm(ca, lnag_ref[...], lnab_ref[...])).astype(BF16)
    tail_a = zpa_ref[:, n_rows:n_rows + HIST_A_PAD, :]
    zpa_ref[:, 0:HIST_A_PAD, :] = tail_a
    newa_ref[...] = tail_a

    proj_ref[:, COL_G:COL_END] = _dot(h_ref[...], win_ref[:, COL_G:COL_END])

    zb = proj_ref[:, COL_B + W_BR:COL_B + 2 * W_BR] * proj_ref[:, COL_B + 2 * W_BR:COL_C]
    zpb_ref[:, HIST_B_PAD:HIST_B_PAD + n_rows, :] = zb.reshape(n_seq, n_rows, W_BR)
    cb = _causal_dwconv(zpb_ref, None, cbw_ref, CONV_B, HIST_B_PAD - (CONV_B - 1), n_seq, n_rows)
    yb_in = (proj_ref[:, COL_B:COL_B + W_BR] * cb.reshape(r_total, W_BR)).astype(BF16)
    tail_b = zpb_ref[:, n_rows:n_rows + HIST_B_PAD, :]
    zpb_ref[:, 0:HIST_B_PAD, :] = tail_b
    newb_ref[...] = tail_b

    vn = _layernorm(jax.nn.gelu(proj_ref[:, COL_C + W_BR:COL_G]), lncg_ref[...], lncb_ref[...])
    if vn_ref is not None:
        vn_ref[...] = vn
    gu = jax.nn.gelu(proj_ref[:, COL_C:COL_C + W_BR])

    ya = _dot(ya_in, waout_ref[...])
    yb = _dot(yb_in, wbout_ref[...])
    mixed = _spatial_mix(vn.astype(BF16), wsp_ref, bsp_ref, n_seq, n_rows)
    yc = _dot((gu * mixed).astype(BF16), wcout_ref[...])

    gates = jax.nn.sigmoid(proj_ref[:, COL_G:COL_END] + gbias_ref[...])
    merged = (gates[:, :D_MODEL] * ya + gates[:, D_MODEL:2 * D_MODEL] * yb
              + gates[:, 2 * D_MODEL:] * yc)
    o_ref[...] = x_ref[...] + _dot(merged.astype(BF16), wo_ref[...])


N_MIXER_PARAMS = 16


def _mixer_kernel(n_seq, n_rows, tiles_per_seq, emit_v, x_ref, hista_ref, histb_ref, *refs):
    params = refs[:N_MIXER_PARAMS]
    o_ref, newa_ref, newb_ref = refs[N_MIXER_PARAMS:N_MIXER_PARAMS + 3]
    rest = refs[N_MIXER_PARAMS + 3:]
    vn_ref = rest[0] if emit_v else None
    _mixer_tile(n_seq, n_rows, pl.program_id(0) % tiles_per_seq == 0, x_ref, hista_ref, histb_ref, params,
                o_ref, newa_ref, newb_ref, vn_ref, *rest[1 if emit_v else 0:])


class _Cast(NamedTuple):
    src: jax.Array
    layer: int


def _cast_chunks(rows, n_steps):
    units = rows // BF16_ROWS
    n_chunks = max(d for d in range(1, min(units, n_steps) + 1) if units % d == 0)
    return rows // n_chunks, n_chunks


def _with_casts(body, n_in, n_out, n_casts, *refs):
    ins, cast_ins = refs[:n_in], refs[n_in:n_in + n_casts]
    outs = refs[n_in + n_casts:n_in + n_casts + n_out]
    cast_outs = refs[n_in + n_casts + n_out:n_in + 2 * n_casts + n_out]
    body(*ins, *outs, *refs[n_in + 2 * n_casts + n_out:])
    for src_ref, dst_ref in zip(cast_ins, cast_outs):
        dst_ref[...] = src_ref[...].astype(BF16)


def _fused_call(body, name, grid, operands, in_specs, out_shapes, out_specs, scratch, casts):
    n_steps = 1
    for g in grid:
        n_steps *= g
    linear = (lambda t: t) if len(grid) == 1 else (lambda b, t: b * grid[1] + t)
    c_in, c_out, c_shapes = [], [], []
    for c in casts:
        _, rows, cols = c.src.shape
        chunk, n_chunks = _cast_chunks(rows, n_steps)
        at = lambda *g, n=n_chunks: jnp.minimum(linear(*g), n - 1)
        c_in.append(pl.BlockSpec((None, chunk, cols), lambda *g, at=at, layer=c.layer: (layer, at(*g), 0)))
        c_out.append(pl.BlockSpec((chunk, cols), lambda *g, at=at: (at(*g), 0)))
        c_shapes.append(jax.ShapeDtypeStruct((rows, cols), BF16))
    outs = pl.pallas_call(
        functools.partial(_with_casts, body, len(operands), len(out_shapes), len(casts)),
        out_shape=tuple(out_shapes) + tuple(c_shapes),
        grid=grid,
        in_specs=list(in_specs) + c_in,
        out_specs=tuple(out_specs) + tuple(c_out),
        scratch_shapes=scratch,
        compiler_params=pltpu.CompilerParams(dimension_semantics=("arbitrary",) * len(grid),
                                             vmem_limit_bytes=VMEM_LIMIT_BYTES),
        name=name,
    )(*operands, *[c.src for c in casts])
    return outs[:len(out_shapes)], outs[len(out_shapes):]


def _layer_spec(layer, tail_shape):
    zeros = (0,) * len(tail_shape)
    return pl.BlockSpec((None,) + tuple(tail_shape), lambda *g: (layer,) + zeros,
                        pipeline_mode=pl.Buffered(1))


def _resident_spec(shape):
    return pl.BlockSpec(tuple(shape), lambda *g: (0, 0), pipeline_mode=pl.Buffered(1))


def _tile_spec(rows, width, n_prompt_steps):
    return pl.BlockSpec((rows, width), lambda t: (jnp.minimum(t, n_prompt_steps - 1), 0))


def _whole_spec(shape):
    return pl.BlockSpec(tuple(shape), lambda t: (0,) * len(shape))


def _ffn_call(xp, xs, layer, w, wb, tm, casts):
    n_p = xp.shape[0] // tm
    outs, cast_out = _fused_call(
        functools.partial(_ffn_kernel, n_p), "ffn1", (n_p + 1,),
        [xp, xs, w['ffn1_norm'], wb['ffn1_w_gate'], wb['ffn1_w_up'], wb['ffn1_w_down']],
        [_tile_spec(tm, D_MODEL, n_p), _whole_spec(xs.shape), _layer_spec(layer, (1, D_MODEL)),
         _resident_spec((D_MODEL, D_FF)), _resident_spec((D_MODEL, D_FF)), _resident_spec((D_FF, D_MODEL))],
        [jax.ShapeDtypeStruct(xp.shape, F32), jax.ShapeDtypeStruct(xs.shape, F32)],
        [_tile_spec(tm, D_MODEL, n_p), _whole_spec(xs.shape)], [], casts)
    return outs, cast_out


def _ffn_ple_call(xp, xs, pp, ps, layer, w, wb, tm, final, casts):
    n_p = xp.shape[0] // tm
    outs, cast_out = _fused_call(
        functools.partial(_ffn_ple_kernel, final, n_p), "ffn2_ple", (n_p + 1,),
        [xp, xs, pp, ps, w['ffn2_norm'], wb['ffn2_w_gate'], wb['ffn2_w_up'], wb['ffn2_w_down'],
         w['ple_norm'], wb['w_ple_gate'], wb['w_ple_proj'], w['final_norm']],
        [_tile_spec(tm, D_MODEL, n_p), _whole_spec(xs.shape),
         pl.BlockSpec((None, tm, D_PLE), lambda t: (layer, jnp.minimum(t, n_p - 1), 0)),
         pl.BlockSpec((None,) + ps.shape[1:], lambda t: (layer, 0, 0)),
         _layer_spec(layer, (1, D_MODEL)), _resident_spec((D_MODEL, D_FF)),
         _resident_spec((D_MODEL, D_FF)), _resident_spec((D_FF, D_MODEL)),
         _layer_spec(layer, (1, D_MODEL)), _resident_spec((D_MODEL, D_MODEL)),
         _resident_spec((D_PLE, D_MODEL)), _resident_spec((1, D_MODEL))],
        [jax.ShapeDtypeStruct(xp.shape, F32), jax.ShapeDtypeStruct(xs.shape, F32)],
        [_tile_spec(tm, D_MODEL, n_p), _whole_spec(xs.shape)], [], casts)
    return outs, cast_out


def _mixer_call(x, hist_a, hist_b, layer, w, wb, seq, tile_rows, emit_v, casts):
    rows = min(seq, tile_rows)
    seqs = tile_rows // rows
    tiles_per_seq = seq // rows
    n_steps = x.shape[0] // tile_rows
    ls = functools.partial(_layer_spec, layer)
    x_spec = lambda width: pl.BlockSpec((tile_rows, width), lambda t: (t, 0))
    state = lambda pad: pl.BlockSpec((seqs, pad, W_BR), lambda t: (t // tiles_per_seq, 0, 0))
    in_specs = [
        x_spec(D_MODEL), state(HIST_A_PAD), state(HIST_B_PAD),
        ls((1, D_MODEL)), _resident_spec((D_MODEL, COL_END)), ls((1, COL_END - COL_G)),
        ls((CONV_A, W_BR)), ls((1, W_BR)), ls((1, W_BR)), ls((1, W_BR)), _resident_spec((W_BR, D_MODEL)),
        ls((CONV_B, W_BR)), _resident_spec((W_BR, D_MODEL)),
        ls((1, W_BR)), ls((1, W_BR)), ls((H_C, MLP_CHUNK, MLP_CHUNK)), ls((MLP_CHUNK, H_C)),
        _resident_spec((W_BR, D_MODEL)), _resident_spec((D_MODEL, D_MODEL)),
    ]
    f32 = lambda *shape: jax.ShapeDtypeStruct(shape, F32)
    out_shapes = [f32(*x.shape), f32(*hist_a.shape), f32(*hist_b.shape)]
    out_specs = [x_spec(D_MODEL), state(HIST_A_PAD), state(HIST_B_PAD)]
    if emit_v:
        out_shapes.append(f32(x.shape[0], W_BR))
        out_specs.append(x_spec(W_BR))
    scratch = [pltpu.VMEM((seqs, HIST_A_PAD + rows, W_BR), F32),
               pltpu.VMEM((seqs, HIST_B_PAD + rows, W_BR), F32),
               pltpu.VMEM((SUBLANES - 1, seqs, HIST_A_PAD - SUBLANES + rows, W_BR), F32),
               pltpu.VMEM((tile_rows, D_MODEL), BF16), pltpu.VMEM((tile_rows, COL_END), F32)]
    params = [w['mix_norm'], wb['w_in'], w['gate_bias'],
              w['conv_a_w'], w['conv_a_b'], w['ln_a_g'], w['ln_a_b'], wb['w_a_out'],
              w['conv_b_w'], wb['w_b_out'], w['ln_c_g'], w['ln_c_b'], w['w_spatial'], w['b_spatial'],
              wb['w_c_out'], wb['w_o']]
    assert len(params) == N_MIXER_PARAMS
    outs, cast_out = _fused_call(
        functools.partial(_mixer_kernel, seqs, rows, tiles_per_seq, emit_v), "mixer",
        (n_steps,), [x, hist_a, hist_b] + params, in_specs, out_shapes, out_specs, scratch, casts)
    return (*outs, None)[:4], cast_out


FFN1_WEIGHTS = ('ffn1_w_gate', 'ffn1_w_up', 'ffn1_w_down')
MIXER_WEIGHTS = ('w_in', 'w_a_out', 'w_b_out', 'w_c_out', 'w_o')
FFN2_WEIGHTS = ('ffn2_w_gate', 'ffn2_w_up', 'ffn2_w_down', 'w_ple_gate', 'w_ple_proj')
CAST_ONLY_STEPS = 16


MIXER_TILE_ROWS = 256
FFN_TILE_ROWS = 256


def _trunk(xp, xs, pp, ps, hist_ap, hist_as, hist_bp, hist_bs, w, raw, depth, seq, dec_seq):
    casts = lambda names, layer: [_Cast(raw[n], layer) for n in names] if layer < depth else []
    wb = [dict() for _ in range(depth)]
    _, done = _fused_call(lambda: None, "cast", (CAST_ONLY_STEPS,), [], [], [], [], [], casts(FFN1_WEIGHTS, 0))
    wb[0].update(zip(FFN1_WEIGHTS, done))
    states, chunk_v = [], []
    for i in range(depth):
        (xp, xs), done = _ffn_call(xp, xs, i, w, wb[i], FFN_TILE_ROWS, casts(MIXER_WEIGHTS, i))
        wb[i].update(zip(MIXER_WEIGHTS, done))
        (xp, nap, nbp, _), done = _mixer_call(xp, hist_ap[i], hist_bp[i], i, w, wb[i], seq, MIXER_TILE_ROWS,
                                              False, casts(FFN2_WEIGHTS, i))
        wb[i].update(zip(FFN2_WEIGHTS, done))
        (xs, nas, nbs, vn), _ = _mixer_call(xs, hist_as[i], hist_bs[i], i, w, wb[i], dec_seq, MIXER_TILE_ROWS,
                                            True, [])
        (xp, xs), done = _ffn_ple_call(xp, xs, pp, ps, i, w, wb[i], FFN_TILE_ROWS, i == depth - 1,
                                       casts(FFN1_WEIGHTS, i + 1))
        if done:
            wb[i + 1].update(zip(FFN1_WEIGHTS, done))
        states.append([nap[:, HIST_A_PAD - (CONV_A - 1):], nas[:, HIST_A_PAD - (CONV_A - 1):],
                       nbp[:, HIST_B_PAD - (CONV_B - 1):], nbs[:, HIST_B_PAD - (CONV_B - 1):]])
        chunk_v.append(vn)
    return (xp, xs, *[jnp.stack(s) for s in zip(*states)], jnp.stack(chunk_v))


def _front_pad(hist, pad_to):
    return jnp.pad(hist, ((0, 0), (0, 0), (pad_to - hist.shape[2], 0), (0, 0)))


def kernel(x_prompt, x_sample, p_prompt, p_sample, cache_conv_a, cache_conv_b, ffn1_norm, ffn1_w_gate, ffn1_w_up, ffn1_w_down, mix_norm, w_in, gate_bias, conv_a_w, conv_a_b, ln_a_g, ln_a_b, w_a_out, conv_b_w, w_b_out, ln_c_g, ln_c_b, w_spatial, b_spatial, w_c_out, w_o, ffn2_norm, ffn2_w_gate, ffn2_w_up, ffn2_w_down, ple_norm, w_ple_gate, w_ple_proj, final_norm):
    depth = w_in.shape[0]
    batch, seq, _ = x_prompt.shape
    dec_batch, dec_seq, _ = x_sample.shape
    row = lambda a: a.reshape(a.shape[0], 1, a.shape[1])
    w = dict(
        ffn1_norm=row(ffn1_norm), mix_norm=row(mix_norm), gate_bias=row(gate_bias),
        conv_a_w=conv_a_w, conv_a_b=row(conv_a_b), ln_a_g=row(ln_a_g), ln_a_b=row(ln_a_b),
        conv_b_w=conv_b_w, ln_c_g=row(ln_c_g), ln_c_b=row(ln_c_b), w_spatial=w_spatial,
        b_spatial=jnp.swapaxes(b_spatial, 1, 2), ffn2_norm=row(ffn2_norm), ple_norm=row(ple_norm),
        final_norm=final_norm.reshape(1, D_MODEL))
    raw = dict(ffn1_w_gate=ffn1_w_gate, ffn1_w_up=ffn1_w_up, ffn1_w_down=ffn1_w_down,
               w_in=w_in, w_a_out=w_a_out, w_b_out=w_b_out, w_c_out=w_c_out, w_o=w_o,
               ffn2_w_gate=ffn2_w_gate, ffn2_w_up=ffn2_w_up, ffn2_w_down=ffn2_w_down,
               w_ple_gate=w_ple_gate, w_ple_proj=w_ple_proj)

    (y_prompt, y_sample, conv_a_prompt, conv_a_sample, conv_b_prompt, conv_b_sample, chunk_v) = _trunk(
        x_prompt.reshape(batch * seq, D_MODEL), x_sample.reshape(dec_batch * dec_seq, D_MODEL),
        p_prompt.reshape(depth, batch * seq, D_PLE), p_sample.reshape(depth, dec_batch * dec_seq, D_PLE),
        jnp.zeros((depth, batch, HIST_A_PAD, W_BR), F32), _front_pad(cache_conv_a, HIST_A_PAD),
        jnp.zeros((depth, batch, HIST_B_PAD, W_BR), F32), _front_pad(cache_conv_b, HIST_B_PAD),
        w, raw, depth, seq, dec_seq)

    return (y_prompt.reshape(x_prompt.shape), y_sample.reshape(x_sample.shape),
            conv_a_prompt, conv_a_sample, conv_b_prompt, conv_b_sample,
            chunk_v.reshape(depth, dec_batch, dec_seq, W_BR))
```
